```python
import jax, jax.numpy as jnp
from jax import lax
import numpy as np

D_MODEL = 1024
BATCH = 4
SEQ = 4096
DEPTH = 4

BRANCH_W = D_MODEL // 2
N_BRANCHES = 3
POOL_WINDOWS = (2, 4, 8, 16)
POOL_GROUPS = 4
POOL_GROUP_W = BRANCH_W // POOL_GROUPS
RET_HEADS = 4
RET_HEAD_DIM = BRANCH_W // RET_HEADS
RET_CHUNK = 128
ROPE_BASE = 10000.0
RWKV_HEAD_DIM = 64
RWKV_HEADS = BRANCH_W // RWKV_HEAD_DIM
RWKV_W_LORA = 64
RWKV_A_LORA = 64
RWKV_SHIFT_W = 3 * BRANCH_W + RWKV_W_LORA + RWKV_A_LORA
IN_SPLIT_SIZES = (BRANCH_W, BRANCH_W, BRANCH_W, BRANCH_W, BRANCH_W, BRANCH_W,
                  RWKV_SHIFT_W, BRANCH_W, N_BRANCHES * D_MODEL)
D_IN = sum(IN_SPLIT_SIZES)
NORM_EPS = 1e-6
RET_NORM_EPS = 1e-5
RWKV_NORM_EPS = 64e-5

kernel_name = "hybrid_pool_retention_rwkv7_gated_parallel"


def rms_norm(x, g):
    x32 = x.astype(jnp.float32)
    y = x32 * lax.rsqrt(jnp.mean(x32 * x32, axis=-1, keepdims=True) + NORM_EPS)
    return (y * g).astype(x.dtype)


def head_norm(x, eps):
    x32 = x.astype(jnp.float32)
    mu = jnp.mean(x32, axis=-1, keepdims=True)
    xc = x32 - mu
    var = jnp.mean(xc * xc, axis=-1, keepdims=True)
    return xc * lax.rsqrt(var + eps)


def causal_pool_mixer(u, pool_w, pool_scale):
    B, T, _ = u.shape
    u32 = u.astype(jnp.float32)
    cs_pad = jnp.pad(jnp.cumsum(u32, axis=1), ((0, 0), (1, 0), (0, 0)))
    t_idx = jnp.arange(T)
    outs = []
    for gi, w in enumerate(POOL_WINDOWS):
        sl = slice(gi * POOL_GROUP_W, (gi + 1) * POOL_GROUP_W)
        c = cs_pad[..., sl]
        lower = jnp.pad(c[:, :T + 1 - w], ((0, 0), (w - 1, 0), (0, 0)))
        win_sum = c[:, 1:] - lower
        count = jnp.minimum(t_idx + 1, w).astype(jnp.float32)[None, :, None]
        outs.append(win_sum / count - u32[..., sl])
    d = jnp.stack(outs, axis=2).astype(u.dtype)
    y = jnp.einsum('btgc,gcd->btgd', d, pool_w).reshape(B, T, BRANCH_W)
    return y * pool_scale


def rotary(x, positions):
    half = x.shape[-1] // 2
    freqs = ROPE_BASE ** (-jnp.arange(half, dtype=jnp.float32) / half)
    ang = positions.astype(jnp.float32)[..., None] * freqs
    cos = jnp.cos(ang)[:, :, None, :]
    sin = jnp.sin(ang)[:, :, None, :]
    x32 = x.astype(jnp.float32)
    x1, x2 = x32[..., :half], x32[..., half:]
    return jnp.concatenate([x1 * cos - x2 * sin, x1 * sin + x2 * cos], axis=-1)


def retention(q, k, v, positions, ret_norm_g):
    dtype = q.dtype
    B, T, _ = q.shape
    H, dh, C = RET_HEADS, RET_HEAD_DIM, RET_CHUNK
    N = T // C
    qh = rotary(q.reshape(B, T, H, dh), positions)
    kh = rotary(k.reshape(B, T, H, dh), positions) * (dh ** -0.5)
    vh = v.reshape(B, T, H, dh).astype(jnp.float32)

    def to_chunks(a):
        return a.reshape(B, N, C, H, dh).transpose(0, 3, 1, 2, 4)

    qc, kc, vc = to_chunks(qh), to_chunks(kh), to_chunks(vh)
    log_gamma = jnp.log1p(-(2.0 ** (-5.0 - jnp.arange(H, dtype=jnp.float32))))
    i = jnp.arange(C, dtype=jnp.float32)
    diff = i[:, None] - i[None, :]
    decay_mask = jnp.where(diff >= 0,
                           jnp.exp(log_gamma[:, None, None] * jnp.maximum(diff, 0.0)),
                           0.0)
    scores = jnp.einsum('bhnid,bhnjd->bhnij', qc, kc) * decay_mask[None, :, None]
    o_intra = jnp.einsum('bhnij,bhnje->bhnie', scores, vc)
    k_tail = kc * jnp.exp(log_gamma[:, None] * (C - 1 - i))[None, :, None, :, None]
    kv = jnp.einsum('bhnjd,bhnje->bhnde', k_tail, vc)
    chunk_decay = jnp.exp(log_gamma * C)[None, :, None, None]

    def step(R, kv_n):
        return chunk_decay * R + kv_n, R

    _, R_prev = lax.scan(step, jnp.zeros((B, H, dh, dh), jnp.float32), jnp.moveaxis(kv, 2, 0))
    R_prev = jnp.moveaxis(R_prev, 0, 2)
    q_dec = qc * jnp.exp(log_gamma[:, None] * (i + 1.0))[None, :, None, :, None]
    o = o_intra + jnp.einsum('bhnid,bhnde->bhnie', q_dec, R_prev)
    o = o.transpose(0, 2, 3, 1, 4).reshape(B, T, H, dh)
    o = head_norm(o, RET_NORM_EPS).reshape(B, T, BRANCH_W) * ret_norm_g
    return o.astype(dtype)


def rwkv7_time_mix(p, shift_mu, w0, w2, a0, a2, k_k, k_a, r_k, ln_w, ln_b):
    dtype = p.dtype
    B, T, _ = p.shape
    H, N = RWKV_HEADS, RWKV_HEAD_DIM
    p_prev = jnp.pad(p[:, :-1], ((0, 0), (1, 0), (0, 0)))
    p = p + (p_prev - p) * shift_mu
    split_pts = (BRANCH_W, BRANCH_W + RWKV_W_LORA, 2 * BRANCH_W + RWKV_W_LORA,
                 3 * BRANCH_W + RWKV_W_LORA)
    r, wl, k, v, al = jnp.split(p, split_pts, axis=-1)
    w = -jax.nn.softplus(-(w0 + jnp.tanh(wl) @ w2)) - 0.5
    decay = jnp.exp(-jnp.exp(w.astype(jnp.float32)))
    a = jax.nn.sigmoid(a0 + al @ a2)

    def heads(z):
        return z.reshape(B, T, H, N).astype(jnp.float32)

    kk = heads(k * k_k)
    kk = kk / jnp.maximum(jnp.sqrt(jnp.sum(kk * kk, axis=-1, keepdims=True)), 1e-12)
    k = k * (1.0 + (a - 1.0) * k_a)
    r_h, k_h, v_h, a_h, w_h = heads(r), heads(k), heads(v), heads(a), heads(decay)
    xs = tuple(jnp.moveaxis(z, 1, 0) for z in (r_h, w_h, k_h, v_h, -kk, kk * a_h))

    def step(S, inp):
        r_t, w_t, k_t, v_t, a_t, b_t = inp
        sa = jnp.einsum('bhvk,bhk->bhv', S, a_t)
        S = (S * w_t[:, :, None, :] + sa[..., None] * b_t[:, :, None, :]
             + v_t[..., None] * k_t[:, :, None, :])
        return S, jnp.einsum('bhvk,bhk->bhv', S, r_t)

    _, y = lax.scan(step, jnp.zeros((B, H, N, N), jnp.float32), xs)
    y = jnp.moveaxis(y, 0, 1)
    y = head_norm(y, RWKV_NORM_EPS).reshape(B, T, BRANCH_W) * ln_w + ln_b
    bonus = jnp.sum(r_h * k_h * r_k, axis=-1, keepdims=True) * v_h
    y = y + bonus.reshape(B, T, BRANCH_W)
    return y.astype(dtype)


def hybrid_layer(x, c_act, positions, norm_g, w_ada, b_ada, w_in, pool_w, pool_scale,
                 ret_norm_g, shift_mu, w0, w2, a0, a2, k_k, k_a, r_k, ln_w, ln_b,
                 w_branch, w_out):
    B, T, D = x.shape
    mod = c_act @ w_ada + b_ada
    shift, scale, gate = jnp.split(mod[:, None, :], 3, axis=-1)
    h = rms_norm(x, norm_g) * (1.0 + scale) + shift
    proj = h @ w_in
    points = np.cumsum(IN_SPLIT_SIZES)[:-1].tolist()
    pool_u, pool_g, q, k, v, ret_g, rw, rw_g, gate_logits = jnp.split(proj, points, axis=-1)
    o_pool = causal_pool_mixer(pool_u, pool_w, pool_scale) * jax.nn.silu(pool_g)
    o_ret = retention(q, k, v, positions, ret_norm_g) * jax.nn.silu(ret_g)
    o_rwkv = rwkv7_time_mix(rw, shift_mu, w0, w2, a0, a2, k_k, k_a, r_k,
                            ln_w, ln_b) * jax.nn.silu(rw_g)
    branches = jnp.stack([o_pool, o_ret, o_rwkv], axis=2)
    y = jnp.einsum('btnw,nwd->btnd', branches, w_branch)
    gates = jax.nn.sigmoid(gate_logits.reshape(B, T, N_BRANCHES, D))
    merged = jnp.sum(gates * y, axis=2)
    return x + gate * (merged @ w_out)


def setup_inputs(seed: int = 0) -> dict:
    key = jax.random.key(seed)
    ks = jax.random.split(key, 24)
    L, D, W = DEPTH, D_MODEL, BRANCH_W
    nrm = jax.random.normal
    f32 = jnp.float32
    return {
        "x": nrm(ks[0], (BATCH, SEQ, D), f32),
        "c": nrm(ks[1], (BATCH, D), f32),
        "positions": jnp.broadcast_to(jnp.arange(SEQ, dtype=jnp.int32), (BATCH, SEQ)),
        "norm_g": 1.0 + 0.05 * nrm(ks[2], (L, D), f32),
        "w_ada": 0.5 * D ** -0.5 * nrm(ks[3], (L, D, 3 * D), f32),
        "b_ada": 0.01 * nrm(ks[4], (L, 3 * D), f32),
        "w_in": D ** -0.5 * nrm(ks[5], (L, D, D_IN), f32),
        "pool_w": POOL_GROUP_W ** -0.5 * nrm(ks[6], (L, POOL_GROUPS, POOL_GROUP_W, POOL_GROUP_W), f32),
        "pool_scale": 1.0 + 0.1 * nrm(ks[7], (L, W), f32),
        "ret_norm_g": 1.0 + 0.05 * nrm(ks[8], (L, W), f32),
        "rwkv_shift_mu": jax.random.uniform(ks[9], (L, RWKV_SHIFT_W), f32),
        "rwkv_w0": jax.random.uniform(ks[10], (L, W), f32, -6.0, 1.0),
        "rwkv_w2": 0.5 * RWKV_W_LORA ** -0.5 * nrm(ks[11], (L, RWKV_W_LORA, W), f32),
        "rwkv_a0": 0.1 * nrm(ks[12], (L, W), f32),
        "rwkv_a2": 0.5 * RWKV_A_LORA ** -0.5 * nrm(ks[13], (L, RWKV_A_LORA, W), f32),
        "rwkv_k_k": 0.85 + 0.05 * nrm(ks[14], (L, W), f32),
        "rwkv_k_a": 1.0 + 0.05 * nrm(ks[15], (L, W), f32),
        "rwkv_r_k": 0.1 * nrm(ks[16], (L, RWKV_HEADS, RWKV_HEAD_DIM), f32),
        "rwkv_ln_w": 1.0 + 0.05 * nrm(ks[17], (L, W), f32),
        "rwkv_ln_b": 0.01 * nrm(ks[18], (L, W), f32),
        "w_branch": W ** -0.5 * nrm(ks[19], (L, N_BRANCHES, W, D), f32),
        "w_out": D ** -0.5 * nrm(ks[20], (L, D, D), f32),
        "final_g": 1.0 + 0.05 * nrm(ks[21], (D,), f32),
    }


def reference(x, c, positions, norm_g, w_ada, b_ada, w_in, pool_w, pool_scale, ret_norm_g,
              rwkv_shift_mu, rwkv_w0, rwkv_w2, rwkv_a0, rwkv_a2, rwkv_k_k, rwkv_k_a,
              rwkv_r_k, rwkv_ln_w, rwkv_ln_b, w_branch, w_out, final_g):
    c_act = jax.nn.silu(c)
    for l in range(DEPTH):
        x = hybrid_layer(x, c_act, positions, norm_g[l], w_ada[l], b_ada[l], w_in[l],
                         pool_w[l], pool_scale[l], ret_norm_g[l], rwkv_shift_mu[l],
                         rwkv_w0[l], rwkv_w2[l], rwkv_a0[l], rwkv_a2[l], rwkv_k_k[l],
                         rwkv_k_a[l], rwkv_r_k[l], rwkv_ln_w[l], rwkv_ln_b[l],
                         w_branch[l], w_out[l])
    return rms_norm(x, final_g)
```

```python
import functools
import math

import numpy as np
import jax
import jax.numpy as jnp
from jax import lax
from jax.experimental import pallas as pl
from jax.experimental.pallas import tpu as pltpu

F32 = jnp.float32
BF16 = jnp.bfloat16

D_MODEL = 1024
BRANCH_W = D_MODEL // 2
N_BRANCHES = 3
POOL_WINDOWS = (2, 4, 8, 16)
POOL_GROUP_W = BRANCH_W // len(POOL_WINDOWS)
POOL_HALO = 16
RET_HEADS = 4
RET_HEAD_DIM = BRANCH_W // RET_HEADS
RET_CHUNK = 128
ROPE_BASE = 10000.0
RWKV_HEAD_DIM = 64
RWKV_HEADS = BRANCH_W // RWKV_HEAD_DIM
RWKV_LORA = 64
RWKV_CHUNK = 64
RWKV_PAIR_W = 2 * RWKV_HEAD_DIM
RWKV_PAIRS = RWKV_HEADS // 2
NORM_EPS = 1e-6
RET_NORM_EPS = 1e-5
RWKV_NORM_EPS = 64e-5

COL_POOL_U, COL_POOL_G = 0, 1
COL_RET_Q, COL_RET_K, COL_RET_V, COL_RET_G = 2, 3, 4, 5
COL_RW_R, COL_RW_K, COL_RW_V, COL_RW_G = 6, 7, 8, 9
COL_GATES_1024 = 5
COL_LORA_128 = 64
D_IN = 10 * BRANCH_W + N_BRANCHES * D_MODEL + 2 * RWKV_LORA

VMEM_LIMIT_BYTES = 56 * 1024 * 1024

IN_TM, IN_TN = 1024, 640
POOL_TT = 512
MERGE_TM = 256
ROPE_TT = 1024


def _dot(a, b):
    return jnp.dot(a.astype(BF16), b.astype(BF16), preferred_element_type=F32)


def _dot_nt(a, b):
    return lax.dot_general(a.astype(BF16), b.astype(BF16), (((1,), (1,)), ((), ())),
                           preferred_element_type=F32)


def _split3(x):
    hi = x.astype(BF16)
    r1 = x - hi.astype(F32)
    mid = r1.astype(BF16)
    lo = (r1 - mid.astype(F32)).astype(BF16)
    return hi, mid, lo


def _dot_exact_rhs(x, m_bf16):
    hi, mid, lo = _split3(x)
    acc = jnp.dot(lo, m_bf16, preferred_element_type=F32)
    acc = acc + jnp.dot(mid, m_bf16, preferred_element_type=F32)
    return acc + jnp.dot(hi, m_bf16, preferred_element_type=F32)


def _dot_exact_lhs(m_bf16, x):
    hi, mid, lo = _split3(x)
    acc = jnp.dot(m_bf16, lo, preferred_element_type=F32)
    acc = acc + jnp.dot(m_bf16, mid, preferred_element_type=F32)
    return acc + jnp.dot(m_bf16, hi, preferred_element_type=F32)


def _dot_hilo(a, b):
    a_hi = a.astype(BF16)
    a_lo = (a - a_hi.astype(F32)).astype(BF16)
    b_hi = b.astype(BF16)
    b_lo = (b - b_hi.astype(F32)).astype(BF16)
    acc = jnp.dot(a_lo, b_hi, preferred_element_type=F32)
    acc = acc + jnp.dot(a_hi, b_lo, preferred_element_type=F32)
    return acc + jnp.dot(a_hi, b_hi, preferred_element_type=F32)


def _silu(x):
    return x * jax.nn.sigmoid(x)


def _params(*sem):
    return pltpu.CompilerParams(dimension_semantics=sem, vmem_limit_bytes=VMEM_LIMIT_BYTES)


def _mod_kernel(c_ref, w_ref, b_ref, o_ref):
    o_ref[0] = _dot(_silu(c_ref[...]), w_ref[0]) + b_ref[0]


def _modulation(c_pad, w_ada, b_ada):
    depth, d, d3 = w_ada.shape
    rows = c_pad.shape[0]
    return pl.pallas_call(
        _mod_kernel,
        out_shape=jax.ShapeDtypeStruct((depth, rows, d3), F32),
        grid=(depth, d3 // d),
        in_specs=[
            pl.BlockSpec((rows, d), lambda l, j: (0, 0)),
            pl.BlockSpec((1, d, d), lambda l, j: (l, 0, j)),
            pl.BlockSpec((1, 1, d), lambda l, j: (l, 0, j)),
        ],
        out_specs=pl.BlockSpec((1, rows, d), lambda l, j: (l, 0, j)),
        compiler_params=_params("arbitrary", "arbitrary"),
        name="adaln_mod",
    )(c_pad, w_ada, b_ada.reshape(depth, 1, d3))


def _rope_kernel(pos_ref, freq_ref, sign_ref, cos_ref, sin_ref):
    ang = pos_ref[...].astype(F32) * freq_ref[...]
    cos_ref[...] = jnp.cos(ang)
    sin_ref[...] = jnp.sin(ang) * sign_ref[...]


def _rope_tables(pos_col):
    ntok = pos_col.shape[0]
    half = RET_HEAD_DIM // 2
    freqs = ROPE_BASE ** (-jnp.arange(half, dtype=F32) / half)
    freq2 = jnp.concatenate([freqs, freqs]).reshape(1, RET_HEAD_DIM)
    sign = jnp.concatenate([-jnp.ones((half,), F32), jnp.ones((half,), F32)]).reshape(1, RET_HEAD_DIM)
    vec = pl.BlockSpec((1, RET_HEAD_DIM), lambda i: (0, 0))
    tab = pl.BlockSpec((ROPE_TT, RET_HEAD_DIM), lambda i: (i, 0))
    return pl.pallas_call(
        _rope_kernel,
        out_shape=(jax.ShapeDtypeStruct((ntok, RET_HEAD_DIM), F32),) * 2,
        grid=(ntok // ROPE_TT,),
        in_specs=[pl.BlockSpec((ROPE_TT, 1), lambda i: (i, 0)), vec, vec],
        out_specs=(tab, tab),
        compiler_params=_params("arbitrary"),
        name="rope_tables",
    )(pos_col, freq2, sign)


def _inproj_kernel(x_ref, g_ref, shift_ref, scale_ref, w_ref, o_ref, h_ref):
    @pl.when(pl.program_id(1) == 0)
    def _():
        x = x_ref[...]
        y = x * lax.rsqrt(jnp.mean(x * x, axis=-1, keepdims=True) + NORM_EPS) * g_ref[...]
        h_ref[...] = (y * (1.0 + scale_ref[0]) + shift_ref[0]).astype(BF16)

    o_ref[...] = jnp.dot(h_ref[...], w_ref[...], preferred_element_type=F32)


def _in_projection(x2, norm_g, shift, scale, w_in_bf16, seq):
    ntok, d = x2.shape
    tiles_per_seq = seq // IN_TM
    vec = pl.BlockSpec((1, 1, d), lambda i, j: (i // tiles_per_seq, 0, 0))
    return pl.pallas_call(
        _inproj_kernel,
        out_shape=jax.ShapeDtypeStruct((ntok, D_IN), F32),
        grid=(ntok // IN_TM, D_IN // IN_TN),
        in_specs=[
            pl.BlockSpec((IN_TM, d), lambda i, j: (i, 0)),
            pl.BlockSpec((1, d), lambda i, j: (0, 0)),
            vec, vec,
            pl.BlockSpec((d, IN_TN), lambda i, j: (0, j)),
        ],
        out_specs=pl.BlockSpec((IN_TM, IN_TN), lambda i, j: (i, j)),
        scratch_shapes=[pltpu.VMEM((IN_TM, d), BF16)],
        compiler_params=_params("arbitrary", "arbitrary"),
        name="in_projection",
    )(x2, norm_g.reshape(1, d), shift, scale, w_in_bf16)


def _pool_kernel(u_ref, g_ref, w_ref, sc_ref, o_ref, ext_ref):
    t = pl.program_id(1)
    tt = u_ref.shape[0]

    @pl.when(t == 0)
    def _():
        ext_ref[0:POOL_HALO, :] = jnp.zeros((POOL_HALO, BRANCH_W), F32)

    @pl.when(t > 0)
    def _():
        ext_ref[0:POOL_HALO, :] = ext_ref[tt:tt + POOL_HALO, :]

    u = u_ref[...]
    ext_ref[POOL_HALO:POOL_HALO + tt, :] = u
    pos = lax.broadcasted_iota(jnp.int32, (tt, POOL_GROUP_W), 0) + t * tt
    outs = []
    for gi, win in enumerate(POOL_WINDOWS):
        c0 = gi * POOL_GROUP_W
        acc = u[:, c0:c0 + POOL_GROUP_W]
        for k in range(1, win):
            acc = acc + ext_ref[POOL_HALO - k:POOL_HALO - k + tt, c0:c0 + POOL_GROUP_W]
        count = jnp.minimum(pos + 1, win).astype(F32)
        d = acc / count - u[:, c0:c0 + POOL_GROUP_W]
        outs.append(_dot(d, w_ref[gi]))
    y = jnp.concatenate(outs, axis=1) * sc_ref[...]
    o_ref[...] = y * _silu(g_ref[...])


def _pool_mixer(proj, pool_w_bf16, pool_scale, batch, seq):
    ntok = proj.shape[0]
    tiles = seq // POOL_TT
    blk = lambda col: pl.BlockSpec((POOL_TT, BRANCH_W), lambda b, t: (b * tiles + t, col))
    return pl.pallas_call(
        _pool_kernel,
        out_shape=jax.ShapeDtypeStruct((ntok, BRANCH_W), F32),
        grid=(batch, tiles),
        in_specs=[
            blk(COL_POOL_U), blk(COL_POOL_G),
            pl.BlockSpec(pool_w_bf16.shape, lambda b, t: (0, 0, 0)),
            pl.BlockSpec((1, BRANCH_W), lambda b, t: (0, 0)),
        ],
        out_specs=blk(0),
        scratch_shapes=[pltpu.VMEM((POOL_TT + POOL_HALO, BRANCH_W), F32)],
        compiler_params=_params("arbitrary", "arbitrary"),
        name="pool_mixer",
    )(proj, proj, pool_w_bf16, pool_scale.reshape(1, BRANCH_W))


def _ret_log_gamma():
    return [math.log1p(-(2.0 ** (-5.0 - h))) for h in range(RET_HEADS)]


def _ret_kernel(q_ref, k_ref, v_ref, g_ref, cos_ref, sin_ref, mask_ref, qd_ref, kt_ref,
                ng_ref, o_ref, r_ref):
    @pl.when(pl.program_id(1) == 0)
    def _():
        r_ref[...] = jnp.zeros(r_ref.shape, F32)

    cosf = cos_ref[...]
    sinf = sin_ref[...]
    dh = RET_HEAD_DIM
    outs = []
    for h, lg in enumerate(_ret_log_gamma()):
        sl = slice(h * dh, (h + 1) * dh)
        q = q_ref[:, sl]
        k = k_ref[:, sl]
        v = v_ref[:, sl]
        qr = q * cosf + pltpu.roll(q, dh // 2, 1) * sinf
        kr = (k * cosf + pltpu.roll(k, dh // 2, 1) * sinf) * (dh ** -0.5)
        scores = _dot_nt(qr, kr) * mask_ref[h]
        state = r_ref[h]
        o = _dot(scores, v) + _dot(qr * qd_ref[h], state)
        kv = _dot((kr * kt_ref[h]).T, v)
        r_ref[h] = math.exp(lg * RET_CHUNK) * state + kv
        xc = o - jnp.mean(o, axis=-1, keepdims=True)
        var = jnp.mean(xc * xc, axis=-1, keepdims=True)
        outs.append(xc * lax.rsqrt(var + RET_NORM_EPS))
    o = jnp.concatenate(outs, axis=1) * ng_ref[...]
    o_ref[...] = o * _silu(g_ref[...])


def _retention(proj, cosf, sinf, ret_norm_g, batch, seq):
    ntok = proj.shape[0]
    c = RET_CHUNK
    chunks = seq // c
    lg = jnp.asarray(_ret_log_gamma(), F32)
    i = jnp.arange(c, dtype=F32)
    diff = i[:, None] - i[None, :]
    mask = jnp.where(diff >= 0, jnp.exp(lg[:, None, None] * jnp.maximum(diff, 0.0)), 0.0)
    ones = jnp.ones((1, 1, RET_HEAD_DIM), F32)
    q_dec = jnp.exp(lg[:, None] * (i + 1.0))[:, :, None] * ones
    k_tail = jnp.exp(lg[:, None] * (c - 1 - i))[:, :, None] * ones
    blk = lambda col: pl.BlockSpec((c, BRANCH_W), lambda b, n: (b * chunks + n, col))
    tab = pl.BlockSpec((c, RET_HEAD_DIM), lambda b, n: (b * chunks + n, 0))
    const3 = lambda a: pl.BlockSpec(a.shape, lambda b, n: (0, 0, 0))
    return pl.pallas_call(
        _ret_kernel,
        out_shape=jax.ShapeDtypeStruct((ntok, BRANCH_W), F32),
        grid=(batch, chunks),
        in_specs=[
            blk(COL_RET_Q), blk(COL_RET_K), blk(COL_RET_V), blk(COL_RET_G),
            tab, tab, const3(mask), const3(q_dec), const3(k_tail),
            pl.BlockSpec((1, BRANCH_W), lambda b, n: (0, 0)),
        ],
        out_specs=blk(0),
        scratch_shapes=[pltpu.VMEM((RET_HEADS, RET_HEAD_DIM, RET_HEAD_DIM), F32)],
        compiler_params=_params("arbitrary", "arbitrary"),
        name="retention",
    )(proj, proj, proj, proj, cosf, sinf, mask, q_dec, k_tail, ret_norm_g.reshape(1, BRANCH_W))


def _inv_unit_lower(m, row, col):
    same = lambda n: (row // n) == (col // n)
    eye = (row == col).astype(F32)
    md = jnp.where(same(8), m, 0.0)
    m2 = _dot(md, md)
    m4 = _dot(m2, m2)
    t = _dot(eye + md, _dot(eye + m2, eye + m4))
    n = 8
    while n < RWKV_CHUNK:
        off = jnp.where(same(2 * n) & jnp.logical_not(same(n)), m, 0.0)
        t = t + _dot(t, _dot(off, t))
        n *= 2
    return t


def _rwkv_kernel(pr_ref, pk_ref, pv_ref, pg_ref, plo_ref,
                 mur_ref, muk_ref, muv_ref, mulo_ref,
                 w0_ref, w2_ref, a0_ref, a2_ref, kk_ref, ka_ref, rk_ref, lnw_ref, lnb_ref,
                 seg_ref, o_ref,
                 z_ref, prev_r, prev_k, prev_v, prev_lo):
    L = RWKV_CHUNK
    hd = RWKV_HEAD_DIM
    pw = RWKV_PAIR_W

    @pl.when(pl.program_id(1) == 0)
    def _():
        z_ref[...] = jnp.zeros(z_ref.shape, F32)
        prev_r[...] = jnp.zeros(prev_r.shape, F32)
        prev_k[...] = jnp.zeros(prev_k.shape, F32)
        prev_v[...] = jnp.zeros(prev_v.shape, F32)
        prev_lo[...] = jnp.zeros(prev_lo.shape, F32)

    def shifted(p_ref, prev_ref, mu_ref):
        p = p_ref[...]
        first = lax.broadcasted_iota(jnp.int32, p.shape, 0) == 0
        p_prev = jnp.where(first, prev_ref[...], pltpu.roll(p, 1, 0))
        prev_ref[...] = p[L - 1:L, :]
        return p + (p_prev - p) * mu_ref[...]

    r = shifted(pr_ref, prev_r, mur_ref)
    k = shifted(pk_ref, prev_k, muk_ref)
    v = shifted(pv_ref, prev_v, muv_ref)
    lo = shifted(plo_ref, prev_lo, mulo_ref)

    seg = seg_ref[...]
    seg_sum = lambda x: _dot_exact_rhs(x, seg)

    zw = w0_ref[...] + _dot_hilo(jnp.tanh(lo), w2_ref[...])
    neg = -zw
    softplus = jnp.maximum(neg, 0.0) + jnp.log1p(jnp.exp(-jnp.abs(neg)))
    logw = -jnp.exp(-softplus - 0.5)
    lr = jax.nn.sigmoid(a0_ref[...] + _dot_hilo(lo, a2_ref[...]))
    kk = k * kk_ref[...]
    kk = kk / jnp.maximum(jnp.sqrt(seg_sum(kk * kk)), 1e-12)
    k = k * (1.0 + (lr - 1.0) * ka_ref[...])
    a_n = -kk
    b_n = kk * lr

    ti = lax.broadcasted_iota(jnp.int32, (L, L), 0)
    tj = lax.broadcasted_iota(jnp.int32, (L, L), 1)
    tri = (ti >= tj).astype(BF16)
    c = _dot_exact_lhs(tri, logw)
    c_end = c[L - 1:L, :]
    e_c = jnp.exp(c)
    e_nc = jnp.exp(-c)
    e_tail = jnp.exp(c_end - c)
    w_end = jnp.exp(c_end)
    a_t = a_n * jnp.exp(c - logw)
    b_t = b_n * e_nc
    k_t = k * e_nc
    r_t = r * e_c
    b_h = b_n * e_tail
    k_h = k * e_tail

    lane = lax.broadcasted_iota(jnp.int32, (1, pw), 1)
    m_lo = (lane < hd).astype(F32)
    m_hi = 1.0 - m_lo
    row = lax.broadcasted_iota(jnp.int32, (pw, pw), 0)
    col = lax.broadcasted_iota(jnp.int32, (pw, pw), 1)
    strict = (row % L) > (col % L)
    incl = (row % L) >= (col % L)
    eye = row == col
    zeros = jnp.zeros((pw, pw), F32)

    ys = []
    for p in range(RWKV_PAIRS):
        sl = slice(p * pw, (p + 1) * pw)

        def stack(x):
            xs = x[:, sl]
            return jnp.concatenate([xs * m_lo, xs * m_hi], axis=0)

        a_s, b_s, k_s, r_s = stack(a_t), stack(b_t), stack(k_t), stack(r_t)
        bh_s, kh_s, v_s = stack(b_h), stack(k_h), stack(v)
        mm = _dot_nt(jnp.concatenate([a_s, r_s], axis=0), jnp.concatenate([b_s, k_s], axis=0))
        m_ab = jnp.where(strict, mm[0:pw, 0:pw], 0.0)
        m_ak = jnp.where(strict, mm[0:pw, pw:2 * pw], 0.0)
        a_rb = jnp.where(incl, mm[pw:2 * pw, 0:pw], 0.0)
        a_rk = jnp.where(incl, mm[pw:2 * pw, pw:2 * pw], 0.0)
        t_inv = _inv_unit_lower(m_ab, row, col)
        x = _dot(t_inv, jnp.concatenate([a_s, _dot(m_ak, v_s)], axis=1))
        rhs = jnp.concatenate([x, jnp.concatenate([zeros, v_s], axis=1)], axis=0)
        lhs = jnp.concatenate([jnp.concatenate([bh_s.T, kh_s.T], axis=1),
                               jnp.concatenate([a_rb, a_rk], axis=1)], axis=0)
        out = _dot(lhs, rhs)
        phi = out[0:pw, 0:pw] + jnp.where(eye, jnp.broadcast_to(w_end[:, sl], (pw, pw)), 0.0)
        g = out[0:pw, pw:2 * pw]
        q = r_s + out[pw:2 * pw, 0:pw]
        y0 = out[pw:2 * pw, pw:2 * pw]
        z = z_ref[p]
        y_s = _dot(q, z) + y0
        z_ref[p] = _dot(phi, z) + g
        ys.append(y_s[0:L] + y_s[L:2 * L])

    y = jnp.concatenate(ys, axis=1)
    xc = y - seg_sum(y) * (1.0 / hd)
    var = seg_sum(xc * xc) * (1.0 / hd)
    y = xc * lax.rsqrt(var + RWKV_NORM_EPS) * lnw_ref[...] + lnb_ref[...]
    y = y + seg_sum(r * k * rk_ref[...]) * v
    o_ref[...] = y * _silu(pg_ref[...])


def _rwkv_mixer(proj, prm, batch, seq):
    ntok = proj.shape[0]
    L = RWKV_CHUNK
    chunks = seq // L
    w = BRANCH_W
    blk = lambda col: pl.BlockSpec((L, w), lambda b, n: (b * chunks + n, col))
    vec = lambda width: pl.BlockSpec((1, width), lambda b, n: (0, 0))
    mat = lambda a: pl.BlockSpec(a.shape, lambda b, n: (0, 0))
    head = jnp.arange(w) // RWKV_HEAD_DIM
    seg = (head[:, None] == head[None, :]).astype(BF16)
    row = lambda a: a.reshape(1, -1)
    return pl.pallas_call(
        _rwkv_kernel,
        out_shape=jax.ShapeDtypeStruct((ntok, w), F32),
        grid=(batch, chunks),
        in_specs=[
            blk(COL_RW_R), blk(COL_RW_K), blk(COL_RW_V), blk(COL_RW_G),
            pl.BlockSpec((L, 2 * RWKV_LORA), lambda b, n: (b * chunks + n, COL_LORA_128)),
            vec(w), vec(w), vec(w), vec(2 * RWKV_LORA),
            vec(w), mat(prm["w2"]), vec(w), mat(prm["a2"]),
            vec(w), vec(w), vec(w), vec(w), vec(w),
            mat(seg),
        ],
        out_specs=blk(0),
        scratch_shapes=[
            pltpu.VMEM((RWKV_PAIRS, RWKV_PAIR_W, RWKV_PAIR_W), F32),
            pltpu.VMEM((1, w), F32), pltpu.VMEM((1, w), F32), pltpu.VMEM((1, w), F32),
            pltpu.VMEM((1, 2 * RWKV_LORA), F32),
        ],
        compiler_params=_params("arbitrary", "arbitrary"),
        name="rwkv7_mixer",
    )(proj, proj, proj, proj, proj,
      row(prm["mu_r"]), row(prm["mu_k"]), row(prm["mu_v"]), row(prm["mu_lo"]),
      row(prm["w0"]), prm["w2"], row(prm["a0"]), prm["a2"],
      row(prm["k_k"]), row(prm["k_a"]), row(prm["r_k"]), row(prm["ln_w"]), row(prm["ln_b"]),
      seg)


def _merge_kernel(op_ref, or_ref, ow_ref, g0_ref, g1_ref, g2_ref, wb_ref, wo_ref, x_ref,
                  gate_ref, fg_ref, o_ref, *, final_norm):
    merged = None
    for n, (br_ref, gl_ref) in enumerate(((op_ref, g0_ref), (or_ref, g1_ref), (ow_ref, g2_ref))):
        term = jax.nn.sigmoid(gl_ref[...]) * _dot(br_ref[...], wb_ref[n])
        merged = term if merged is None else merged + term
    out = x_ref[...] + gate_ref[0] * _dot(merged, wo_ref[...])
    if final_norm:
        out = out * lax.rsqrt(jnp.mean(out * out, axis=-1, keepdims=True) + NORM_EPS) * fg_ref[...]
    o_ref[...] = out


def _merge(o_pool, o_ret, o_rwkv, proj, w_branch_bf16, w_out_bf16, x2, gate, final_g, seq,
           final_norm):
    ntok, d = x2.shape
    tm = MERGE_TM
    tiles_per_seq = seq // tm
    br = pl.BlockSpec((tm, BRANCH_W), lambda i: (i, 0))
    gl = lambda n: pl.BlockSpec((tm, d), lambda i: (i, COL_GATES_1024 + n))
    xs = pl.BlockSpec((tm, d), lambda i: (i, 0))
    return pl.pallas_call(
        functools.partial(_merge_kernel, final_norm=final_norm),
        out_shape=jax.ShapeDtypeStruct((ntok, d), F32),
        grid=(ntok // tm,),
        in_specs=[
            br, br, br, gl(0), gl(1), gl(2),
            pl.BlockSpec(w_branch_bf16.shape, lambda i: (0, 0, 0)),
            pl.BlockSpec(w_out_bf16.shape, lambda i: (0, 0)),
            xs,
            pl.BlockSpec((1, 1, d), lambda i: (i // tiles_per_seq, 0, 0)),
            pl.BlockSpec((1, d), lambda i: (0, 0)),
        ],
        out_specs=xs,
        compiler_params=_params("arbitrary"),
        name="merge_out",
    )(o_pool, o_ret, o_rwkv, proj, proj, proj, w_branch_bf16, w_out_bf16, x2, gate,
      final_g.reshape(1, d))


def _permute_w_in(w_in):
    w = BRANCH_W
    rw = 6 * w
    r = w_in[..., rw:rw + w]
    wl = w_in[..., rw + w:rw + w + RWKV_LORA]
    k = w_in[..., rw + w + RWKV_LORA:rw + 2 * w + RWKV_LORA]
    v = w_in[..., rw + 2 * w + RWKV_LORA:rw + 3 * w + RWKV_LORA]
    al = w_in[..., rw + 3 * w + RWKV_LORA:rw + 3 * w + 2 * RWKV_LORA]
    rest = w_in[..., rw + 3 * w + 2 * RWKV_LORA:]
    return jnp.concatenate([w_in[..., :rw], r, k, v, rest, wl, al], axis=-1)


def kernel(x, c, positions, norm_g, w_ada, b_ada, w_in, pool_w, pool_scale, ret_norm_g,
           rwkv_shift_mu, rwkv_w0, rwkv_w2, rwkv_a0, rwkv_a2, rwkv_k_k, rwkv_k_a, rwkv_r_k,
           rwkv_ln_w, rwkv_ln_b, w_branch, w_out, final_g):
    batch, seq, d = x.shape
    depth = w_in.shape[0]
    ntok = batch * seq
    w = BRANCH_W
    assert d == D_MODEL and w_in.shape[-1] == D_IN
    assert seq % IN_TM == 0 and seq % POOL_TT == 0 and seq % RET_CHUNK == 0

    mod_rows = 8
    c_pad = jnp.zeros((mod_rows, d), F32).at[:batch].set(c)
    mod = _modulation(c_pad, w_ada, b_ada)[:, :batch]
    cosf, sinf = _rope_tables(positions.reshape(ntok, 1))

    w_in_p = _permute_w_in(w_in).astype(BF16)
    pool_w_b = pool_w.astype(BF16)
    w_branch_b = w_branch.astype(BF16)
    w_out_b = w_out.astype(BF16)
    lora_pad = jnp.zeros((depth, RWKV_LORA, w), F32)
    w2_pad = jnp.concatenate([rwkv_w2, lora_pad], axis=1)
    a2_pad = jnp.concatenate([lora_pad, rwkv_a2], axis=1)
    mu = rwkv_shift_mu
    mu_lo = jnp.concatenate([mu[:, w:w + RWKV_LORA], mu[:, 3 * w + RWKV_LORA:]], axis=1)

    x2 = x.reshape(ntok, d)
    for l in range(depth):
        shift = mod[l, :, 0:d].reshape(batch, 1, d)
        scale = mod[l, :, d:2 * d].reshape(batch, 1, d)
        gate = mod[l, :, 2 * d:].reshape(batch, 1, d)
        proj = _in_projection(x2, norm_g[l], shift, scale, w_in_p[l], seq)
        o_pool = _pool_mixer(proj, pool_w_b[l], pool_scale[l], batch, seq)
        o_ret = _retention(proj, cosf, sinf, ret_norm_g[l], batch, seq)
        prm = dict(mu_r=mu[l, 0:w], mu_k=mu[l, w + RWKV_LORA:2 * w + RWKV_LORA],
                   mu_v=mu[l, 2 * w + RWKV_LORA:3 * w + RWKV_LORA], mu_lo=mu_lo[l],
                   w0=rwkv_w0[l], w2=w2_pad[l], a0=rwkv_a0[l], a2=a2_pad[l],
                   k_k=rwkv_k_k[l], k_a=rwkv_k_a[l], r_k=rwkv_r_k[l],
                   ln_w=rwkv_ln_w[l], ln_b=rwkv_ln_b[l])
        o_rwkv = _rwkv_mixer(proj, prm, batch, seq)
        x2 = _merge(o_pool, o_ret, o_rwkv, proj, w_branch_b[l], w_out_b[l], x2, gate, final_g,
                    seq, final_norm=(l == depth - 1))
    return x2.reshape(batch, seq, d)
```

```python
import functools
import math

import jax
import jax.numpy as jnp
from jax import lax
from jax.experimental import pallas as pl
from jax.experimental.pallas import tpu as pltpu

F32 = jnp.float32
BF16 = jnp.bfloat16

D_MODEL = 1024
BRANCH_W = D_MODEL // 2
N_BRANCHES = 3
POOL_WINDOWS = (2, 4, 8, 16)
POOL_GROUP_W = BRANCH_W // len(POOL_WINDOWS)
POOL_HALO = 16
RET_HEADS = 4
RET_HEAD_DIM = BRANCH_W // RET_HEADS
RET_CHUNK = 128
ROPE_BASE = 10000.0
RWKV_HEAD_DIM = 64
RWKV_HEADS = BRANCH_W // RWKV_HEAD_DIM
RWKV_LORA = 64
RWKV_CHUNK = 64
RWKV_GROUP = 4
RWKV_PAIR_W = 2 * RWKV_HEAD_DIM
RWKV_PAIRS = RWKV_HEADS // 2
NORM_EPS = 1e-6
RET_NORM_EPS = 1e-5
RWKV_NORM_EPS = 64e-5

COL_POOL_U, COL_POOL_G = 0, 1
COL_RET_Q, COL_RET_K, COL_RET_V, COL_RET_G = 2, 3, 4, 5
COL_RW_R, COL_RW_K, COL_RW_V, COL_RW_G = 6, 7, 8, 9
COL_GATES_1024 = 5
COL_LORA_128 = 64
D_IN = 10 * BRANCH_W + N_BRANCHES * D_MODEL + 2 * RWKV_LORA

VMEM_LIMIT_BYTES = 56 * 1024 * 1024

IN_TM, IN_TN = 1024, 640
POOL_TT = 512
MERGE_TM = 256
ROPE_TT = 1024


def _dot(a, b):
    return jnp.dot(a.astype(BF16), b.astype(BF16), preferred_element_type=F32)


def _dot_nt(a, b):
    return lax.dot_general(a.astype(BF16), b.astype(BF16), (((1,), (1,)), ((), ())),
                           preferred_element_type=F32)


def _bmm(a, b):
    return lax.dot_general(a.astype(BF16), b.astype(BF16), (((2,), (1,)), ((0,), (0,))),
                           preferred_element_type=F32)


def _bmm_nt(a, b):
    return lax.dot_general(a.astype(BF16), b.astype(BF16), (((2,), (2,)), ((0,), (0,))),
                           preferred_element_type=F32)


def _split3(x):
    hi = x.astype(BF16)
    r1 = x - hi.astype(F32)
    mid = r1.astype(BF16)
    lo = (r1 - mid.astype(F32)).astype(BF16)
    return hi, mid, lo


def _dot_exact_rhs(x, m_bf16):
    hi, mid, lo = _split3(x)
    acc = jnp.dot(lo, m_bf16, preferred_element_type=F32)
    acc = acc + jnp.dot(mid, m_bf16, preferred_element_type=F32)
    return acc + jnp.dot(hi, m_bf16, preferred_element_type=F32)


def _dot_exact_lhs(m_bf16, x):
    hi, mid, lo = _split3(x)
    acc = jnp.dot(m_bf16, lo, preferred_element_type=F32)
    acc = acc + jnp.dot(m_bf16, mid, preferred_element_type=F32)
    return acc + jnp.dot(m_bf16, hi, preferred_element_type=F32)


def _dot_hilo(a, b):
    a_hi = a.astype(BF16)
    a_lo = (a - a_hi.astype(F32)).astype(BF16)
    b_hi = b.astype(BF16)
    b_lo = (b - b_hi.astype(F32)).astype(BF16)
    acc = jnp.dot(a_lo, b_hi, preferred_element_type=F32)
    acc = acc + jnp.dot(a_hi, b_lo, preferred_element_type=F32)
    return acc + jnp.dot(a_hi, b_hi, preferred_element_type=F32)


def _silu(x):
    return x * jax.nn.sigmoid(x)


def _params(*sem):
    return pltpu.CompilerParams(dimension_semantics=sem, vmem_limit_bytes=VMEM_LIMIT_BYTES)


def _mod_kernel(c_ref, w_ref, b_ref, o_ref):
    o_ref[0] = _dot(_silu(c_ref[...]), w_ref[0]) + b_ref[0]


def _modulation(c_pad, w_ada, b_ada):
    depth, d, d3 = w_ada.shape
    rows = c_pad.shape[0]
    return pl.pallas_call(
        _mod_kernel,
        out_shape=jax.ShapeDtypeStruct((depth, rows, d3), F32),
        grid=(depth, d3 // d),
        in_specs=[
            pl.BlockSpec((rows, d), lambda l, j: (0, 0)),
            pl.BlockSpec((1, d, d), lambda l, j: (l, 0, j)),
            pl.BlockSpec((1, 1, d), lambda l, j: (l, 0, j)),
        ],
        out_specs=pl.BlockSpec((1, rows, d), lambda l, j: (l, 0, j)),
        compiler_params=_params("arbitrary", "arbitrary"),
        name="adaln_mod",
    )(c_pad, w_ada, b_ada.reshape(depth, 1, d3))


def _rope_kernel(pos_ref, freq_ref, sign_ref, cos_ref, sin_ref):
    ang = pos_ref[...].astype(F32) * freq_ref[...]
    cos_ref[...] = jnp.cos(ang)
    sin_ref[...] = jnp.sin(ang) * sign_ref[...]


def _rope_tables(pos_col):
    ntok = pos_col.shape[0]
    half = RET_HEAD_DIM // 2
    freqs = ROPE_BASE ** (-jnp.arange(half, dtype=F32) / half)
    freq2 = jnp.concatenate([freqs, freqs]).reshape(1, RET_HEAD_DIM)
    sign = jnp.concatenate([-jnp.ones((half,), F32), jnp.ones((half,), F32)]).reshape(1, RET_HEAD_DIM)
    vec = pl.BlockSpec((1, RET_HEAD_DIM), lambda i: (0, 0))
    tab = pl.BlockSpec((ROPE_TT, RET_HEAD_DIM), lambda i: (i, 0))
    return pl.pallas_call(
        _rope_kernel,
        out_shape=(jax.ShapeDtypeStruct((ntok, RET_HEAD_DIM), F32),) * 2,
        grid=(ntok // ROPE_TT,),
        in_specs=[pl.BlockSpec((ROPE_TT, 1), lambda i: (i, 0)), vec, vec],
        out_specs=(tab, tab),
        compiler_params=_params("arbitrary"),
        name="rope_tables",
    )(pos_col, freq2, sign)


def _inproj_kernel(x_ref, g_ref, shift_ref, scale_ref, w_ref, o_ref, h_ref):
    @pl.when(pl.program_id(1) == 0)
    def _():
        x = x_ref[...]
        y = x * lax.rsqrt(jnp.mean(x * x, axis=-1, keepdims=True) + NORM_EPS) * g_ref[...]
        h_ref[...] = (y * (1.0 + scale_ref[0]) + shift_ref[0]).astype(BF16)

    o_ref[...] = jnp.dot(h_ref[...], w_ref[...], preferred_element_type=F32)


def _in_projection(x2, norm_g, shift, scale, w_in_bf16, seq):
    ntok, d = x2.shape
    tiles_per_seq = seq // IN_TM
    vec = pl.BlockSpec((1, 1, d), lambda i, j: (i // tiles_per_seq, 0, 0))
    return pl.pallas_call(
        _inproj_kernel,
        out_shape=jax.ShapeDtypeStruct((ntok, D_IN), F32),
        grid=(ntok // IN_TM, D_IN // IN_TN),
        in_specs=[
            pl.BlockSpec((IN_TM, d), lambda i, j: (i, 0)),
            pl.BlockSpec((1, d), lambda i, j: (0, 0)),
            vec, vec,
            pl.BlockSpec((d, IN_TN), lambda i, j: (0, j)),
        ],
        out_specs=pl.BlockSpec((IN_TM, IN_TN), lambda i, j: (i, j)),
        scratch_shapes=[pltpu.VMEM((IN_TM, d), BF16)],
        compiler_params=_params("arbitrary", "arbitrary"),
        name="in_projection",
    )(x2, norm_g.reshape(1, d), shift, scale, w_in_bf16)


def _pool_kernel(u_ref, g_ref, w_ref, sc_ref, o_ref, ext_ref):
    t = pl.program_id(1)
    tt = u_ref.shape[0]

    @pl.when(t == 0)
    def _():
        ext_ref[0:POOL_HALO, :] = jnp.zeros((POOL_HALO, BRANCH_W), F32)

    @pl.when(t > 0)
    def _():
        ext_ref[0:POOL_HALO, :] = ext_ref[tt:tt + POOL_HALO, :]

    u = u_ref[...]
    ext_ref[POOL_HALO:POOL_HALO + tt, :] = u
    pos = lax.broadcasted_iota(jnp.int32, (tt, POOL_GROUP_W), 0) + t * tt
    outs = []
    for gi, win in enumerate(POOL_WINDOWS):
        c0 = gi * POOL_GROUP_W
        acc = u[:, c0:c0 + POOL_GROUP_W]
        for k in range(1, win):
            acc = acc + ext_ref[POOL_HALO - k:POOL_HALO - k + tt, c0:c0 + POOL_GROUP_W]
        count = jnp.minimum(pos + 1, win).astype(F32)
        d = acc / count - u[:, c0:c0 + POOL_GROUP_W]
        outs.append(_dot(d, w_ref[gi]))
    y = jnp.concatenate(outs, axis=1) * sc_ref[...]
    o_ref[...] = y * _silu(g_ref[...])


def _pool_mixer(proj, pool_w_bf16, pool_scale, batch, seq):
    ntok = proj.shape[0]
    tiles = seq // POOL_TT
    blk = lambda col: pl.BlockSpec((POOL_TT, BRANCH_W), lambda b, t: (b * tiles + t, col))
    return pl.pallas_call(
        _pool_kernel,
        out_shape=jax.ShapeDtypeStruct((ntok, BRANCH_W), F32),
        grid=(batch, tiles),
        in_specs=[
            blk(COL_POOL_U), blk(COL_POOL_G),
            pl.BlockSpec(pool_w_bf16.shape, lambda b, t: (0, 0, 0)),
            pl.BlockSpec((1, BRANCH_W), lambda b, t: (0, 0)),
        ],
        out_specs=blk(0),
        scratch_shapes=[pltpu.VMEM((POOL_TT + POOL_HALO, BRANCH_W), F32)],
        compiler_params=_params("arbitrary", "arbitrary"),
        name="pool_mixer",
    )(proj, proj, pool_w_bf16, pool_scale.reshape(1, BRANCH_W))


def _ret_log_gamma():
    return [math.log1p(-(2.0 ** (-5.0 - h))) for h in range(RET_HEADS)]


def _ret_kernel(q_ref, k_ref, v_ref, g_ref, cos_ref, sin_ref, mask_ref, qd_ref, kt_ref,
                ng_ref, o_ref, r_ref):
    @pl.when(pl.program_id(1) == 0)
    def _():
        r_ref[...] = jnp.zeros(r_ref.shape, F32)

    cosf = cos_ref[...]
    sinf = sin_ref[...]
    dh = RET_HEAD_DIM
    outs = []
    for h, lg in enumerate(_ret_log_gamma()):
        sl = slice(h * dh, (h + 1) * dh)
        q = q_ref[:, sl]
        k = k_ref[:, sl]
        v = v_ref[:, sl]
        qr = q * cosf + pltpu.roll(q, dh // 2, 1) * sinf
        kr = (k * cosf + pltpu.roll(k, dh // 2, 1) * sinf) * (dh ** -0.5)
        scores = _dot_nt(qr, kr) * mask_ref[h]
        state = r_ref[h]
        o = _dot(scores, v) + _dot(qr * qd_ref[h], state)
        kv = _dot((kr * kt_ref[h]).T, v)
        r_ref[h] = math.exp(lg * RET_CHUNK) * state + kv
        xc = o - jnp.mean(o, axis=-1, keepdims=True)
        var = jnp.mean(xc * xc, axis=-1, keepdims=True)
        outs.append(xc * lax.rsqrt(var + RET_NORM_EPS))
    o = jnp.concatenate(outs, axis=1) * ng_ref[...]
    o_ref[...] = o * _silu(g_ref[...])


def _retention(proj, cosf, sinf, ret_norm_g, batch, seq):
    ntok = proj.shape[0]
    c = RET_CHUNK
    chunks = seq // c
    lg = jnp.asarray(_ret_log_gamma(), F32)
    i = jnp.arange(c, dtype=F32)
    diff = i[:, None] - i[None, :]
    mask = jnp.where(diff >= 0, jnp.exp(lg[:, None, None] * jnp.maximum(diff, 0.0)), 0.0)
    ones = jnp.ones((1, 1, RET_HEAD_DIM), F32)
    q_dec = jnp.exp(lg[:, None] * (i + 1.0))[:, :, None] * ones
    k_tail = jnp.exp(lg[:, None] * (c - 1 - i))[:, :, None] * ones
    blk = lambda col: pl.BlockSpec((c, BRANCH_W), lambda b, n: (b * chunks + n, col))
    tab = pl.BlockSpec((c, RET_HEAD_DIM), lambda b, n: (b * chunks + n, 0))
    const3 = lambda a: pl.BlockSpec(a.shape, lambda b, n: (0, 0, 0))
    return pl.pallas_call(
        _ret_kernel,
        out_shape=jax.ShapeDtypeStruct((ntok, BRANCH_W), F32),
        grid=(batch, chunks),
        in_specs=[
            blk(COL_RET_Q), blk(COL_RET_K), blk(COL_RET_V), blk(COL_RET_G),
            tab, tab, const3(mask), const3(q_dec), const3(k_tail),
            pl.BlockSpec((1, BRANCH_W), lambda b, n: (0, 0)),
        ],
        out_specs=blk(0),
        scratch_shapes=[pltpu.VMEM((RET_HEADS, RET_HEAD_DIM, RET_HEAD_DIM), F32)],
        compiler_params=_params("arbitrary", "arbitrary"),
        name="retention",
    )(proj, proj, proj, proj, cosf, sinf, mask, q_dec, k_tail, ret_norm_g.reshape(1, BRANCH_W))


def _inv_unit_lower(m, row, col):
    same = lambda n: ((row // n) == (col // n))[None]
    eye = (row == col).astype(F32)[None]
    md = jnp.where(same(8), m, 0.0)
    m2 = _bmm(md, md)
    t = _bmm(_bmm(eye + md, eye + m2), eye + _bmm(m2, m2))
    n = 8
    while n < RWKV_CHUNK:
        off = jnp.where(same(2 * n) & jnp.logical_not(same(n)), m, 0.0)
        t = t + _bmm(t, _bmm(off, t))
        n *= 2
    return t


def _rwkv_kernel(pr_ref, pk_ref, pv_ref, pg_ref, plo_ref,
                 mur_ref, muk_ref, muv_ref, mulo_ref,
                 w0_ref, w2_ref, a0_ref, a2_ref, kk_ref, ka_ref, rk_ref, lnw_ref, lnb_ref,
                 seg_ref, o_ref,
                 z_ref, prev_r, prev_k, prev_v, prev_lo):
    L = RWKV_CHUNK
    G = RWKV_GROUP
    GL = G * L
    hd = RWKV_HEAD_DIM
    pw = RWKV_PAIR_W
    P = RWKV_PAIRS

    @pl.when(pl.program_id(1) == 0)
    def _():
        z_ref[...] = jnp.zeros(z_ref.shape, F32)
        prev_r[...] = jnp.zeros(prev_r.shape, F32)
        prev_k[...] = jnp.zeros(prev_k.shape, F32)
        prev_v[...] = jnp.zeros(prev_v.shape, F32)
        prev_lo[...] = jnp.zeros(prev_lo.shape, F32)

    def shifted(p_ref, prev_ref, mu_ref):
        p = p_ref[...]
        first = lax.broadcasted_iota(jnp.int32, p.shape, 0) == 0
        p_prev = jnp.where(first, prev_ref[...], pltpu.roll(p, 1, 0))
        prev_ref[...] = p[GL - 1:GL, :]
        return p + (p_prev - p) * mu_ref[...]

    r = shifted(pr_ref, prev_r, mur_ref)
    k = shifted(pk_ref, prev_k, muk_ref)
    v = shifted(pv_ref, prev_v, muv_ref)
    lo = shifted(plo_ref, prev_lo, mulo_ref)

    seg = seg_ref[...]
    seg_sum = lambda x: _dot_exact_rhs(x, seg)

    zw = w0_ref[...] + _dot_hilo(jnp.tanh(lo), w2_ref[...])
    neg = -zw
    softplus = jnp.maximum(neg, 0.0) + jnp.log1p(jnp.exp(-jnp.abs(neg)))
    logw = -jnp.exp(-softplus - 0.5)
    lr = jax.nn.sigmoid(a0_ref[...] + _dot_hilo(lo, a2_ref[...]))
    kk = k * kk_ref[...]
    kk = kk / jnp.maximum(jnp.sqrt(seg_sum(kk * kk)), 1e-12)
    k = k * (1.0 + (lr - 1.0) * ka_ref[...])
    a_n = -kk
    b_n = kk * lr

    ti = lax.broadcasted_iota(jnp.int32, (GL, GL), 0)
    tj = lax.broadcasted_iota(jnp.int32, (GL, GL), 1)
    tri = ((ti >= tj) & ((ti // L) == (tj // L))).astype(BF16)
    c = _dot_exact_lhs(tri, logw)
    c_last = [c[(g + 1) * L - 1:(g + 1) * L, :] for g in range(G)]
    c_end = jnp.concatenate([jnp.broadcast_to(cl, (L, BRANCH_W)) for cl in c_last], axis=0)
    e_c = jnp.exp(c)
    e_nc = jnp.exp(-c)
    e_tail = jnp.exp(c_end - c)
    a_t = a_n * jnp.exp(c - logw)
    b_t = b_n * e_nc
    k_t = k * e_nc
    r_t = r * e_c
    b_h = b_n * e_tail
    k_h = k * e_tail

    lane = lax.broadcasted_iota(jnp.int32, (pw, pw), 1)
    row = lax.broadcasted_iota(jnp.int32, (pw, pw), 0)
    head_mask = ((lane < hd) == (row < L)).astype(F32)
    strict = ((row % L) > (lane % L))[None]
    incl = ((row % L) >= (lane % L))[None]
    eye = (row == lane)[None]

    def stack(x):
        parts = []
        for g in range(G):
            for p in range(P):
                xs = x[g * L:(g + 1) * L, p * pw:(p + 1) * pw]
                parts.append(jnp.concatenate([xs, xs], axis=0) * head_mask)
        return jnp.stack(parts, axis=0)

    a_s, b_s, k_s, r_s = stack(a_t), stack(b_t), stack(k_t), stack(r_t)
    bh_s, kh_s, v_s = stack(b_h), stack(k_h), stack(v)
    w_end = jnp.stack([jnp.exp(c_last[g][:, p * pw:(p + 1) * pw])
                       for g in range(G) for p in range(P)], axis=0)

    mm = _bmm_nt(jnp.concatenate([a_s, r_s], axis=1), jnp.concatenate([b_s, k_s], axis=1))
    m_ab = jnp.where(strict, mm[:, 0:pw, 0:pw], 0.0)
    m_ak = jnp.where(strict, mm[:, 0:pw, pw:2 * pw], 0.0)
    a_rb = jnp.where(incl, mm[:, pw:2 * pw, 0:pw], 0.0)
    a_rk = jnp.where(incl, mm[:, pw:2 * pw, pw:2 * pw], 0.0)
    t_inv = _inv_unit_lower(m_ab, row, lane)
    x = _bmm(t_inv, jnp.concatenate([a_s, _bmm(m_ak, v_s)], axis=2))
    rhs = jnp.concatenate([x, jnp.concatenate([jnp.zeros_like(v_s), v_s], axis=2)], axis=1)
    lhs = jnp.concatenate(
        [jnp.concatenate([jnp.swapaxes(bh_s, 1, 2), jnp.swapaxes(kh_s, 1, 2)], axis=2),
         jnp.concatenate([a_rb, a_rk], axis=2)], axis=1)
    out = _bmm(lhs, rhs)
    phi = out[:, 0:pw, 0:pw] + jnp.where(eye, w_end, 0.0)
    gg = out[:, 0:pw, pw:2 * pw]
    q = r_s + out[:, pw:2 * pw, 0:pw]
    y0 = out[:, pw:2 * pw, pw:2 * pw]

    z = z_ref[...]
    rows = []
    for g in range(G):
        cs = slice(g * P, (g + 1) * P)
        y_s = _bmm(q[cs], z) + y0[cs]
        z = _bmm(phi[cs], z) + gg[cs]
        rows.append(jnp.concatenate([y_s[p, 0:L] + y_s[p, L:2 * L] for p in range(P)], axis=1))
    z_ref[...] = z

    y = jnp.concatenate(rows, axis=0)
    xc = y - seg_sum(y) * (1.0 / hd)
    var = seg_sum(xc * xc) * (1.0 / hd)
    y = xc * lax.rsqrt(var + RWKV_NORM_EPS) * lnw_ref[...] + lnb_ref[...]
    y = y + seg_sum(r * k * rk_ref[...]) * v
    o_ref[...] = y * _silu(pg_ref[...])


def _rwkv_mixer(proj, prm, batch, seq):
    ntok = proj.shape[0]
    rows_per_step = RWKV_GROUP * RWKV_CHUNK
    steps = seq // rows_per_step
    w = BRANCH_W
    blk = lambda col: pl.BlockSpec((rows_per_step, w), lambda b, n: (b * steps + n, col))
    vec = lambda width: pl.BlockSpec((1, width), lambda b, n: (0, 0))
    mat = lambda a: pl.BlockSpec(a.shape, lambda b, n: (0, 0))
    head = jnp.arange(w) // RWKV_HEAD_DIM
    seg = (head[:, None] == head[None, :]).astype(BF16)
    row = lambda a: a.reshape(1, -1)
    return pl.pallas_call(
        _rwkv_kernel,
        out_shape=jax.ShapeDtypeStruct((ntok, w), F32),
        grid=(batch, steps),
        in_specs=[
            blk(COL_RW_R), blk(COL_RW_K), blk(COL_RW_V), blk(COL_RW_G),
            pl.BlockSpec((rows_per_step, 2 * RWKV_LORA),
                         lambda b, n: (b * steps + n, COL_LORA_128)),
            vec(w), vec(w), vec(w), vec(2 * RWKV_LORA),
            vec(w), mat(prm["w2"]), vec(w), mat(prm["a2"]),
            vec(w), vec(w), vec(w), vec(w), vec(w),
            mat(seg),
        ],
        out_specs=blk(0),
        scratch_shapes=[
            pltpu.VMEM((RWKV_PAIRS, RWKV_PAIR_W, RWKV_PAIR_W), F32),
            pltpu.VMEM((1, w), F32), pltpu.VMEM((1, w), F32), pltpu.VMEM((1, w), F32),
            pltpu.VMEM((1, 2 * RWKV_LORA), F32),
        ],
        compiler_params=_params("arbitrary", "arbitrary"),
        name="rwkv7_mixer",
    )(proj, proj, proj, proj, proj,
      row(prm["mu_r"]), row(prm["mu_k"]), row(prm["mu_v"]), row(prm["mu_lo"]),
      row(prm["w0"]), prm["w2"], row(prm["a0"]), prm["a2"],
      row(prm["k_k"]), row(prm["k_a"]), row(prm["r_k"]), row(prm["ln_w"]), row(prm["ln_b"]),
      seg)


def _merge_kernel(op_ref, or_ref, ow_ref, g0_ref, g1_ref, g2_ref, wb_ref, wo_ref, x_ref,
                  gate_ref, fg_ref, o_ref, *, final_norm):
    merged = None
    for n, (br_ref, gl_ref) in enumerate(((op_ref, g0_ref), (or_ref, g1_ref), (ow_ref, g2_ref))):
        term = jax.nn.sigmoid(gl_ref[...]) * _dot(br_ref[...], wb_ref[n])
        merged = term if merged is None else merged + term
    out = x_ref[...] + gate_ref[0] * _dot(merged, wo_ref[...])
    if final_norm:
        out = out * lax.rsqrt(jnp.mean(out * out, axis=-1, keepdims=True) + NORM_EPS) * fg_ref[...]
    o_ref[...] = out


def _merge(o_pool, o_ret, o_rwkv, proj, w_branch_bf16, w_out_bf16, x2, gate, final_g, seq,
           final_norm):
    ntok, d = x2.shape
    tm = MERGE_TM
    tiles_per_seq = seq // tm
    br = pl.BlockSpec((tm, BRANCH_W), lambda i: (i, 0))
    gl = lambda n: pl.BlockSpec((tm, d), lambda i: (i, COL_GATES_1024 + n))
    xs = pl.BlockSpec((tm, d), lambda i: (i, 0))
    return pl.pallas_call(
        functools.partial(_merge_kernel, final_norm=final_norm),
        out_shape=jax.ShapeDtypeStruct((ntok, d), F32),
        grid=(ntok // tm,),
        in_specs=[
            br, br, br, gl(0), gl(1), gl(2),
            pl.BlockSpec(w_branch_bf16.shape, lambda i: (0, 0, 0)),
            pl.BlockSpec(w_out_bf16.shape, lambda i: (0, 0)),
            xs,
            pl.BlockSpec((1, 1, d), lambda i: (i // tiles_per_seq, 0, 0)),
            pl.BlockSpec((1, d), lambda i: (0, 0)),
        ],
        out_specs=xs,
        compiler_params=_params("arbitrary"),
        name="merge_out",
    )(o_pool, o_ret, o_rwkv, proj, proj, proj, w_branch_bf16, w_out_bf16, x2, gate,
      final_g.reshape(1, d))


def _permute_w_in(w_in):
    w = BRANCH_W
    rw = 6 * w
    r = w_in[..., rw:rw + w]
    wl = w_in[..., rw + w:rw + w + RWKV_LORA]
    k = w_in[..., rw + w + RWKV_LORA:rw + 2 * w + RWKV_LORA]
    v = w_in[..., rw + 2 * w + RWKV_LORA:rw + 3 * w + RWKV_LORA]
    al = w_in[..., rw + 3 * w + RWKV_LORA:rw + 3 * w + 2 * RWKV_LORA]
    rest = w_in[..., rw + 3 * w + 2 * RWKV_LORA:]
    return jnp.concatenate([w_in[..., :rw], r, k, v, rest, wl, al], axis=-1)


def kernel(x, c, positions, norm_g, w_ada, b_ada, w_in, pool_w, pool_scale, ret_norm_g,
           rwkv_shift_mu, rwkv_w0, rwkv_w2, rwkv_a0, rwkv_a2, rwkv_k_k, rwkv_k_a, rwkv_r_k,
           rwkv_ln_w, rwkv_ln_b, w_branch, w_out, final_g):
    batch, seq, d = x.shape
    depth = w_in.shape[0]
    ntok = batch * seq
    w = BRANCH_W
    assert d == D_MODEL and w_in.shape[-1] == D_IN
    assert seq % IN_TM == 0 and seq % POOL_TT == 0 and seq % RET_CHUNK == 0
    assert seq % (RWKV_GROUP * RWKV_CHUNK) == 0

    mod_rows = 8
    c_pad = jnp.zeros((mod_rows, d), F32).at[:batch].set(c)
    mod = _modulation(c_pad, w_ada, b_ada)[:, :batch]
    cosf, sinf = _rope_tables(positions.reshape(ntok, 1))

    w_in_p = _permute_w_in(w_in).astype(BF16)
    pool_w_b = pool_w.astype(BF16)
    w_branch_b = w_branch.astype(BF16)
    w_out_b = w_out.astype(BF16)
    lora_pad = jnp.zeros((depth, RWKV_LORA, w), F32)
    w2_pad = jnp.concatenate([rwkv_w2, lora_pad], axis=1)
    a2_pad = jnp.concatenate([lora_pad, rwkv_a2], axis=1)
    mu = rwkv_shift_mu
    mu_lo = jnp.concatenate([mu[:, w:w + RWKV_LORA], mu[:, 3 * w + RWKV_LORA:]], axis=1)

    x2 = x.reshape(ntok, d)
    for l in range(depth):
        shift = mod[l, :, 0:d].reshape(batch, 1, d)
        scale = mod[l, :, d:2 * d].reshape(batch, 1, d)
        gate = mod[l, :, 2 * d:].reshape(batch, 1, d)
        proj = _in_projection(x2, norm_g[l], shift, scale, w_in_p[l], seq)
        o_pool = _pool_mixer(proj, pool_w_b[l], pool_scale[l], batch, seq)
        o_ret = _retention(proj, cosf, sinf, ret_norm_g[l], batch, seq)
        prm = dict(mu_r=mu[l, 0:w], mu_k=mu[l, w + RWKV_LORA:2 * w + RWKV_LORA],
                   mu_v=mu[l, 2 * w + RWKV_LORA:3 * w + RWKV_LORA], mu_lo=mu_lo[l],
                   w0=rwkv_w0[l], w2=w2_pad[l], a0=rwkv_a0[l], a2=a2_pad[l],
                   k_k=rwkv_k_k[l], k_a=rwkv_k_a[l], r_k=rwkv_r_k[l],
                   ln_w=rwkv_ln_w[l], ln_b=rwkv_ln_b[l])
        o_rwkv = _rwkv_mixer(proj, prm, batch, seq)
        x2 = _merge(o_pool, o_ret, o_rwkv, proj, w_branch_b[l], w_out_b[l], x2, gate, final_g,
                    seq, final_norm=(l == depth - 1))
    return x2.reshape(batch, seq, d)
```

```python
import functools
import math

import jax
import jax.numpy as jnp
from jax import lax
from jax.experimental import pallas as pl
from jax.experimental.pallas import tpu as pltpu

F32 = jnp.float32
BF16 = jnp.bfloat16

D_MODEL = 1024
BRANCH_W = D_MODEL // 2
N_BRANCHES = 3
POOL_WINDOWS = (2, 4, 8, 16)
POOL_GROUP_W = BRANCH_W // len(POOL_WINDOWS)
POOL_HALO = 16
RET_HEADS = 4
RET_HEAD_DIM = BRANCH_W // RET_HEADS
RET_CHUNK = 128
ROPE_BASE = 10000.0
RWKV_HEAD_DIM = 64
RWKV_HEADS = BRANCH_W // RWKV_HEAD_DIM
RWKV_LORA = 64
RWKV_CHUNK = 64
RWKV_GROUP = 4
RWKV_PAIR_W = 2 * RWKV_HEAD_DIM
RWKV_PAIRS = RWKV_HEADS // 2
NORM_EPS = 1e-6
RET_NORM_EPS = 1e-5
RWKV_NORM_EPS = 64e-5

COL_POOL_U, COL_POOL_G = 0, 1
COL_RET_Q, COL_RET_K, COL_RET_V, COL_RET_G = 2, 3, 4, 5
COL_RW_R, COL_RW_K, COL_RW_V, COL_RW_G = 6, 7, 8, 9
COL_GATES_1024 = 5
COL_LORA_128 = 64
D_IN = 10 * BRANCH_W + N_BRANCHES * D_MODEL + 2 * RWKV_LORA
MXU_WIDTH = 256
D_IN_PAD = -(-D_IN // MXU_WIDTH) * MXU_WIDTH

VMEM_LIMIT_BYTES = 56 * 1024 * 1024

IN_TM, IN_TN = 1024, 3 * MXU_WIDTH
POOL_TT = 512
RET_GROUP = 4
MERGE_TM = 512
ROPE_TT = 1024


def _dot(a, b):
    return jnp.dot(a.astype(BF16), b.astype(BF16), preferred_element_type=F32)


def _bmm(a, b):
    return lax.dot_general(a.astype(BF16), b.astype(BF16), (((2,), (1,)), ((0,), (0,))),
                           preferred_element_type=F32)


def _bmm_nt(a, b):
    return lax.dot_general(a.astype(BF16), b.astype(BF16), (((2,), (2,)), ((0,), (0,))),
                           preferred_element_type=F32)


def _split3(x):
    hi = x.astype(BF16)
    r1 = x - hi.astype(F32)
    mid = r1.astype(BF16)
    lo = (r1 - mid.astype(F32)).astype(BF16)
    return hi, mid, lo


def _dot_hilo(a, b):
    a_hi = a.astype(BF16)
    a_lo = (a - a_hi.astype(F32)).astype(BF16)
    b_hi = b.astype(BF16)
    b_lo = (b - b_hi.astype(F32)).astype(BF16)
    acc = jnp.dot(a_lo, b_hi, preferred_element_type=F32)
    acc = acc + jnp.dot(a_hi, b_lo, preferred_element_type=F32)
    return acc + jnp.dot(a_hi, b_hi, preferred_element_type=F32)


def _silu(x):
    return x * jax.nn.sigmoid(x)


def _params(*sem):
    return pltpu.CompilerParams(dimension_semantics=sem, vmem_limit_bytes=VMEM_LIMIT_BYTES)


def _mod_kernel(c_ref, w_ref, b_ref, o_ref):
    o_ref[0] = _dot(_silu(c_ref[...]), w_ref[0]) + b_ref[0]


def _modulation(c_pad, w_ada, b_ada):
    depth, d, d3 = w_ada.shape
    rows = c_pad.shape[0]
    return pl.pallas_call(
        _mod_kernel,
        out_shape=jax.ShapeDtypeStruct((depth, rows, d3), F32),
        grid=(depth, d3 // d),
        in_specs=[
            pl.BlockSpec((rows, d), lambda l, j: (0, 0)),
            pl.BlockSpec((1, d, d), lambda l, j: (l, 0, j)),
            pl.BlockSpec((1, 1, d), lambda l, j: (l, 0, j)),
        ],
        out_specs=pl.BlockSpec((1, rows, d), lambda l, j: (l, 0, j)),
        compiler_params=_params("arbitrary", "arbitrary"),
        name="adaln_mod",
    )(c_pad, w_ada, b_ada.reshape(depth, 1, d3))


def _rope_kernel(pos_ref, freq_ref, sign_ref, cos_ref, sin_ref):
    ang = pos_ref[...].astype(F32) * freq_ref[...]
    cos_ref[...] = jnp.cos(ang)
    sin_ref[...] = jnp.sin(ang) * sign_ref[...]


def _rope_tables(pos_col):
    ntok = pos_col.shape[0]
    half = RET_HEAD_DIM // 2
    freqs = ROPE_BASE ** (-jnp.arange(half, dtype=F32) / half)
    freq2 = jnp.concatenate([freqs, freqs]).reshape(1, RET_HEAD_DIM)
    sign = jnp.concatenate([-jnp.ones((half,), F32), jnp.ones((half,), F32)]).reshape(1, RET_HEAD_DIM)
    vec = pl.BlockSpec((1, RET_HEAD_DIM), lambda i: (0, 0))
    tab = pl.BlockSpec((ROPE_TT, RET_HEAD_DIM), lambda i: (i, 0))
    return pl.pallas_call(
        _rope_kernel,
        out_shape=(jax.ShapeDtypeStruct((ntok, RET_HEAD_DIM), F32),) * 2,
        grid=(ntok // ROPE_TT,),
        in_specs=[pl.BlockSpec((ROPE_TT, 1), lambda i: (i, 0)), vec, vec],
        out_specs=(tab, tab),
        compiler_params=_params("arbitrary"),
        name="rope_tables",
    )(pos_col, freq2, sign)


def _inproj_kernel(x_ref, g_ref, shift_ref, scale_ref, w_ref, o_ref, h_ref):
    @pl.when(pl.program_id(1) == 0)
    def _():
        x = x_ref[...]
        y = x * lax.rsqrt(jnp.mean(x * x, axis=-1, keepdims=True) + NORM_EPS) * g_ref[...]
        h_ref[...] = (y * (1.0 + scale_ref[0]) + shift_ref[0]).astype(BF16)

    o_ref[...] = jnp.dot(h_ref[...], w_ref[...], preferred_element_type=F32).astype(BF16)


def _in_projection(x2, norm_g, shift, scale, w_in_bf16, seq):
    ntok, d = x2.shape
    tiles_per_seq = seq // IN_TM
    vec = pl.BlockSpec((1, 1, d), lambda i, j: (i // tiles_per_seq, 0, 0))
    return pl.pallas_call(
        _inproj_kernel,
        out_shape=jax.ShapeDtypeStruct((ntok, D_IN_PAD), BF16),
        grid=(ntok // IN_TM, D_IN_PAD // IN_TN),
        in_specs=[
            pl.BlockSpec((IN_TM, d), lambda i, j: (i, 0)),
            pl.BlockSpec((1, d), lambda i, j: (0, 0)),
            vec, vec,
            pl.BlockSpec((d, IN_TN), lambda i, j: (0, j)),
        ],
        out_specs=pl.BlockSpec((IN_TM, IN_TN), lambda i, j: (i, j)),
        scratch_shapes=[pltpu.VMEM((IN_TM, d), BF16)],
        compiler_params=_params("arbitrary", "arbitrary"),
        name="in_projection",
    )(x2, norm_g.reshape(1, d), shift, scale, w_in_bf16)


def _pool_kernel(u_ref, g_ref, w_ref, sc_ref, o_ref, ext_ref):
    t = pl.program_id(1)
    tt = u_ref.shape[0]

    @pl.when(t == 0)
    def _():
        ext_ref[0:POOL_HALO, :] = jnp.zeros((POOL_HALO, BRANCH_W), F32)

    @pl.when(t > 0)
    def _():
        ext_ref[0:POOL_HALO, :] = ext_ref[tt:tt + POOL_HALO, :]

    u = u_ref[...].astype(F32)
    ext_ref[POOL_HALO:POOL_HALO + tt, :] = u
    pos = lax.broadcasted_iota(jnp.int32, (tt, POOL_GROUP_W), 0) + t * tt
    outs = []
    for gi, win in enumerate(POOL_WINDOWS):
        c0 = gi * POOL_GROUP_W
        acc = u[:, c0:c0 + POOL_GROUP_W]
        for k in range(1, win):
            acc = acc + ext_ref[POOL_HALO - k:POOL_HALO - k + tt, c0:c0 + POOL_GROUP_W]
        count = jnp.minimum(pos + 1, win).astype(F32)
        d = acc / count - u[:, c0:c0 + POOL_GROUP_W]
        outs.append(_dot(d, w_ref[gi]))
    y = jnp.concatenate(outs, axis=1) * sc_ref[...]
    o_ref[...] = (y * _silu(g_ref[...].astype(F32))).astype(BF16)


def _pool_mixer(proj, pool_w_bf16, pool_scale, batch, seq):
    ntok = proj.shape[0]
    tiles = seq // POOL_TT
    blk = lambda col: pl.BlockSpec((POOL_TT, BRANCH_W), lambda b, t: (b * tiles + t, col))
    return pl.pallas_call(
        _pool_kernel,
        out_shape=jax.ShapeDtypeStruct((ntok, BRANCH_W), BF16),
        grid=(batch, tiles),
        in_specs=[
            blk(COL_POOL_U), blk(COL_POOL_G),
            pl.BlockSpec(pool_w_bf16.shape, lambda b, t: (0, 0, 0)),
            pl.BlockSpec((1, BRANCH_W), lambda b, t: (0, 0)),
        ],
        out_specs=blk(0),
        scratch_shapes=[pltpu.VMEM((POOL_TT + POOL_HALO, BRANCH_W), F32)],
        compiler_params=_params("arbitrary", "arbitrary"),
        name="pool_mixer",
    )(proj, proj, pool_w_bf16, pool_scale.reshape(1, BRANCH_W))


def _ret_log_gamma():
    return [math.log1p(-(2.0 ** (-5.0 - h))) for h in range(RET_HEADS)]


def _ret_kernel(q_ref, k_ref, v_ref, g_ref, cos_ref, sin_ref, mask_ref, qd_ref, kt_ref, cd_ref,
                ng_ref, o_ref, r_ref):
    @pl.when(pl.program_id(1) == 0)
    def _():
        r_ref[...] = jnp.zeros(r_ref.shape, F32)

    dh = RET_HEAD_DIM
    c = RET_CHUNK
    nh = RET_HEADS
    blocks = [(g, h) for g in range(RET_GROUP) for h in range(nh)]
    cosf = cos_ref[...]
    sinf = sin_ref[...]

    def heads(ref, rotate):
        x = ref[...].astype(F32)
        parts = []
        for g, h in blocks:
            xs = x[g * c:(g + 1) * c, h * dh:(h + 1) * dh]
            if rotate:
                rows = slice(g * c, (g + 1) * c)
                xs = xs * cosf[rows] + pltpu.roll(xs, dh // 2, 1) * sinf[rows]
            parts.append(xs)
        return jnp.stack(parts, axis=0)

    qr = heads(q_ref, True)
    kr = heads(k_ref, True) * (dh ** -0.5)
    v = heads(v_ref, False)
    o = _bmm(_bmm_nt(qr, kr) * mask_ref[...], v)
    kv = _bmm(jnp.swapaxes(kr * kt_ref[...], 1, 2), v)
    state = r_ref[...]
    before = []
    for g in range(RET_GROUP):
        before.append(state)
        state = cd_ref[...] * state + kv[g * nh:(g + 1) * nh]
    r_ref[...] = state
    o = o + _bmm(qr * qd_ref[...], jnp.concatenate(before, axis=0))
    xc = o - jnp.mean(o, axis=-1, keepdims=True)
    var = jnp.mean(xc * xc, axis=-1, keepdims=True)
    on = xc * lax.rsqrt(var + RET_NORM_EPS)
    rows = [jnp.concatenate([on[g * nh + h] for h in range(nh)], axis=1) for g in range(RET_GROUP)]
    o = jnp.concatenate(rows, axis=0) * ng_ref[...]
    o_ref[...] = (o * _silu(g_ref[...].astype(F32))).astype(BF16)


def _retention(proj, cosf, sinf, ret_norm_g, batch, seq):
    ntok = proj.shape[0]
    c = RET_CHUNK
    rows = RET_GROUP * c
    steps = seq // rows
    lg = jnp.asarray(_ret_log_gamma(), F32)
    i = jnp.arange(c, dtype=F32)
    diff = i[:, None] - i[None, :]
    mask = jnp.where(diff >= 0, jnp.exp(lg[:, None, None] * jnp.maximum(diff, 0.0)), 0.0)
    ones = jnp.ones((1, 1, RET_HEAD_DIM), F32)
    q_dec = jnp.exp(lg[:, None] * (i + 1.0))[:, :, None] * ones
    k_tail = jnp.exp(lg[:, None] * (c - 1 - i))[:, :, None] * ones
    chunk_decay = jnp.exp(lg * c)[:, None, None] * ones
    per_group = lambda a: jnp.tile(a, (RET_GROUP, 1, 1))
    mask, q_dec, k_tail = per_group(mask), per_group(q_dec), per_group(k_tail)
    blk = lambda col: pl.BlockSpec((rows, BRANCH_W), lambda b, n: (b * steps + n, col))
    tab = pl.BlockSpec((rows, RET_HEAD_DIM), lambda b, n: (b * steps + n, 0))
    const3 = lambda a: pl.BlockSpec(a.shape, lambda b, n: (0, 0, 0))
    return pl.pallas_call(
        _ret_kernel,
        out_shape=jax.ShapeDtypeStruct((ntok, BRANCH_W), BF16),
        grid=(batch, steps),
        in_specs=[
            blk(COL_RET_Q), blk(COL_RET_K), blk(COL_RET_V), blk(COL_RET_G),
            tab, tab, const3(mask), const3(q_dec), const3(k_tail), const3(chunk_decay),
            pl.BlockSpec((1, BRANCH_W), lambda b, n: (0, 0)),
        ],
        out_specs=blk(0),
        scratch_shapes=[pltpu.VMEM((RET_HEADS, RET_HEAD_DIM, RET_HEAD_DIM), F32)],
        compiler_params=_params("arbitrary", "arbitrary"),
        name="retention",
    )(proj, proj, proj, proj, cosf, sinf, mask, q_dec, k_tail, chunk_decay,
      ret_norm_g.reshape(1, BRANCH_W))


def _inv_unit_lower(m, row, col):
    same = lambda n: ((row // n) == (col // n))[None]
    eye = (row == col).astype(F32)[None]
    md = jnp.where(same(8), m, 0.0)
    m2 = _bmm(md, md)
    t = _bmm(_bmm(eye + md, eye + m2), eye + _bmm(m2, m2))
    n = 8
    while n < RWKV_CHUNK:
        off = jnp.where(same(2 * n) & jnp.logical_not(same(n)), m, 0.0)
        t = t + _bmm(t, _bmm(off, t))
        n *= 2
    return t


def _rwkv_kernel(pr_ref, pk_ref, pv_ref, pg_ref, plo_ref,
                 mur_ref, muk_ref, muv_ref, mulo_ref,
                 w0_ref, w2_ref, a0_ref, a2_ref, kk_ref, ka_ref, rk_ref, lnw_ref, lnb_ref,
                 seg_ref, o_ref,
                 z_ref, prev_r, prev_k, prev_v, prev_lo):
    L = RWKV_CHUNK
    G = RWKV_GROUP
    GL = G * L
    hd = RWKV_HEAD_DIM
    pw = RWKV_PAIR_W
    P = RWKV_PAIRS

    @pl.when(pl.program_id(1) == 0)
    def _():
        z_ref[...] = jnp.zeros(z_ref.shape, F32)
        prev_r[...] = jnp.zeros(prev_r.shape, F32)
        prev_k[...] = jnp.zeros(prev_k.shape, F32)
        prev_v[...] = jnp.zeros(prev_v.shape, F32)
        prev_lo[...] = jnp.zeros(prev_lo.shape, F32)

    def shifted(p_ref, prev_ref, mu_ref):
        p = p_ref[...].astype(F32)
        first = lax.broadcasted_iota(jnp.int32, p.shape, 0) == 0
        p_prev = jnp.where(first, prev_ref[...], pltpu.roll(p, 1, 0))
        prev_ref[...] = p[GL - 1:GL, :]
        return p + (p_prev - p) * mu_ref[...]

    r = shifted(pr_ref, prev_r, mur_ref)
    k = shifted(pk_ref, prev_k, muk_ref)
    v = shifted(pv_ref, prev_v, muv_ref)
    lo = shifted(plo_ref, prev_lo, mulo_ref)

    seg = seg_ref[...]
    slabs = BRANCH_W // pw

    def seg_sum(x):
        n = x.shape[0]
        xs = jnp.concatenate([x[:, s * pw:(s + 1) * pw] for s in range(slabs)], axis=0)
        hi = xs.astype(BF16)
        lo_part = (xs - hi.astype(F32)).astype(BF16)
        s2 = (jnp.dot(lo_part, seg, preferred_element_type=F32)
              + jnp.dot(hi, seg, preferred_element_type=F32))
        return jnp.concatenate([s2[s * n:(s + 1) * n] for s in range(slabs)], axis=1)

    zw = w0_ref[...] + _dot_hilo(jnp.tanh(lo), w2_ref[...])
    neg = -zw
    softplus = jnp.maximum(neg, 0.0) + jnp.log1p(jnp.exp(-jnp.abs(neg)))
    logw = -jnp.exp(-softplus - 0.5)
    lr = jax.nn.sigmoid(a0_ref[...] + _dot_hilo(lo, a2_ref[...]))
    kk = k * kk_ref[...]
    kk = kk / jnp.maximum(jnp.sqrt(seg_sum(kk * kk)), 1e-12)
    k = k * (1.0 + (lr - 1.0) * ka_ref[...])
    a_n = -kk
    b_n = kk * lr

    ti = lax.broadcasted_iota(jnp.int32, (G, L, L), 1)
    tj = lax.broadcasted_iota(jnp.int32, (G, L, L), 2)
    tri = (ti >= tj).astype(BF16)
    hi, mid, low = _split3(jnp.stack([logw[g * L:(g + 1) * L] for g in range(G)], axis=0))
    tri_dot = lambda piece: lax.dot_general(tri, piece, (((2,), (1,)), ((0,), (0,))),
                                            preferred_element_type=F32)
    c3 = tri_dot(low) + tri_dot(mid) + tri_dot(hi)
    c = jnp.concatenate([c3[g] for g in range(G)], axis=0)
    c_last = [c[(g + 1) * L - 1:(g + 1) * L, :] for g in range(G)]
    c_end = jnp.concatenate([jnp.broadcast_to(cl, (L, BRANCH_W)) for cl in c_last], axis=0)
    e_c = jnp.exp(c)
    e_nc = jnp.exp(-c)
    e_tail = jnp.exp(c_end - c)
    a_t = a_n * jnp.exp(c - logw)
    b_t = b_n * e_nc
    k_t = k * e_nc
    r_t = r * e_c
    b_h = b_n * e_tail
    k_h = k * e_tail

    lane = lax.broadcasted_iota(jnp.int32, (pw, pw), 1)
    row = lax.broadcasted_iota(jnp.int32, (pw, pw), 0)
    head_mask = ((lane < hd) == (row < L)).astype(F32)
    same_head = (row // L) == (lane // L)
    strict = (same_head & ((row % L) > (lane % L)))[None]
    incl = (same_head & ((row % L) >= (lane % L)))[None]
    eye = (row == lane)[None]

    def stack(x, masked=True):
        parts = []
        for g in range(G):
            for p in range(P):
                xs = x[g * L:(g + 1) * L, p * pw:(p + 1) * pw]
                xs = jnp.concatenate([xs, xs], axis=0)
                parts.append(xs * head_mask if masked else xs)
        return jnp.stack(parts, axis=0)

    a_s, r_s = stack(a_t), stack(r_t)
    b_s, k_s = stack(b_t, masked=False), stack(k_t, masked=False)
    bh_s, kh_s, v_s = stack(b_h), stack(k_h), stack(v)
    w_end = jnp.stack([jnp.exp(c_last[g][:, p * pw:(p + 1) * pw])
                       for g in range(G) for p in range(P)], axis=0)

    mm = _bmm_nt(jnp.concatenate([a_s, r_s], axis=1), jnp.concatenate([b_s, k_s], axis=1))
    m_ab = jnp.where(strict, mm[:, 0:pw, 0:pw], 0.0)
    m_ak = jnp.where(strict, mm[:, 0:pw, pw:2 * pw], 0.0)
    a_rb = jnp.where(incl, mm[:, pw:2 * pw, 0:pw], 0.0)
    a_rk = jnp.where(incl, mm[:, pw:2 * pw, pw:2 * pw], 0.0)
    t_inv = _inv_unit_lower(m_ab, row, lane)
    x = _bmm(t_inv, jnp.concatenate([a_s, _bmm(m_ak, v_s)], axis=2))
    rhs = jnp.concatenate([x, jnp.concatenate([jnp.zeros_like(v_s), v_s], axis=2)], axis=1)
    lhs = jnp.concatenate(
        [jnp.concatenate([jnp.swapaxes(bh_s, 1, 2), jnp.swapaxes(kh_s, 1, 2)], axis=2),
         jnp.concatenate([a_rb, a_rk], axis=2)], axis=1)
    out = _bmm(lhs, rhs)
    phi = out[:, 0:pw, 0:pw] + jnp.where(eye, w_end, 0.0)
    gg = out[:, 0:pw, pw:2 * pw]
    q = r_s + out[:, pw:2 * pw, 0:pw]
    y0 = out[:, pw:2 * pw, pw:2 * pw]

    z = z_ref[...]
    rows = []
    for g in range(G):
        cs = slice(g * P, (g + 1) * P)
        y_s = _bmm(q[cs], z) + y0[cs]
        z = _bmm(phi[cs], z) + gg[cs]
        rows.append(jnp.concatenate([y_s[p, 0:L] + y_s[p, L:2 * L] for p in range(P)], axis=1))
    z_ref[...] = z

    y = jnp.concatenate(rows, axis=0)
    xc = y - seg_sum(y) * (1.0 / hd)
    var = seg_sum(xc * xc) * (1.0 / hd)
    y = xc * lax.rsqrt(var + RWKV_NORM_EPS) * lnw_ref[...] + lnb_ref[...]
    y = y + seg_sum(r * k * rk_ref[...]) * v
    o_ref[...] = (y * _silu(pg_ref[...].astype(F32))).astype(BF16)


def _rwkv_mixer(proj, prm, batch, seq):
    ntok = proj.shape[0]
    rows_per_step = RWKV_GROUP * RWKV_CHUNK
    steps = seq // rows_per_step
    w = BRANCH_W
    blk = lambda col: pl.BlockSpec((rows_per_step, w), lambda b, n: (b * steps + n, col))
    vec = lambda width: pl.BlockSpec((1, width), lambda b, n: (0, 0))
    mat = lambda a: pl.BlockSpec(a.shape, lambda b, n: (0, 0))
    head = jnp.arange(RWKV_PAIR_W) // RWKV_HEAD_DIM
    seg = (head[:, None] == head[None, :]).astype(BF16)
    row = lambda a: a.reshape(1, -1)
    return pl.pallas_call(
        _rwkv_kernel,
        out_shape=jax.ShapeDtypeStruct((ntok, w), BF16),
        grid=(batch, steps),
        in_specs=[
            blk(COL_RW_R), blk(COL_RW_K), blk(COL_RW_V), blk(COL_RW_G),
            pl.BlockSpec((rows_per_step, 2 * RWKV_LORA),
                         lambda b, n: (b * steps + n, COL_LORA_128)),
            vec(w), vec(w), vec(w), vec(2 * RWKV_LORA),
            vec(w), mat(prm["w2"]), vec(w), mat(prm["a2"]),
            vec(w), vec(w), vec(w), vec(w), vec(w),
            mat(seg),
        ],
        out_specs=blk(0),
        scratch_shapes=[
            pltpu.VMEM((RWKV_PAIRS, RWKV_PAIR_W, RWKV_PAIR_W), F32),
            pltpu.VMEM((1, w), F32), pltpu.VMEM((1, w), F32), pltpu.VMEM((1, w), F32),
            pltpu.VMEM((1, 2 * RWKV_LORA), F32),
        ],
        compiler_params=_params("arbitrary", "arbitrary"),
        name="rwkv7_mixer",
    )(proj, proj, proj, proj, proj,
      row(prm["mu_r"]), row(prm["mu_k"]), row(prm["mu_v"]), row(prm["mu_lo"]),
      row(prm["w0"]), prm["w2"], row(prm["a0"]), prm["a2"],
      row(prm["k_k"]), row(prm["k_a"]), row(prm["r_k"]), row(prm["ln_w"]), row(prm["ln_b"]),
      seg)


def _merge_kernel(op_ref, or_ref, ow_ref, g0_ref, g1_ref, g2_ref, wb_ref, wo_ref, x_ref,
                  gate_ref, fg_ref, o_ref, *, final_norm):
    merged = None
    for n, (br_ref, gl_ref) in enumerate(((op_ref, g0_ref), (or_ref, g1_ref), (ow_ref, g2_ref))):
        term = jax.nn.sigmoid(gl_ref[...].astype(F32)) * _dot(br_ref[...], wb_ref[n])
        merged = term if merged is None else merged + term
    out = x_ref[...] + gate_ref[0] * _dot(merged, wo_ref[...])
    if final_norm:
        out = out * lax.rsqrt(jnp.mean(out * out, axis=-1, keepdims=True) + NORM_EPS) * fg_ref[...]
    o_ref[...] = out


def _merge(o_pool, o_ret, o_rwkv, proj, w_branch_bf16, w_out_bf16, x2, gate, final_g, seq,
           final_norm):
    ntok, d = x2.shape
    tm = MERGE_TM
    tiles_per_seq = seq // tm
    br = pl.BlockSpec((tm, BRANCH_W), lambda i: (i, 0))
    gl = lambda n: pl.BlockSpec((tm, d), lambda i: (i, COL_GATES_1024 + n))
    xs = pl.BlockSpec((tm, d), lambda i: (i, 0))
    return pl.pallas_call(
        functools.partial(_merge_kernel, final_norm=final_norm),
        out_shape=jax.ShapeDtypeStruct((ntok, d), F32),
        grid=(ntok // tm,),
        in_specs=[
            br, br, br, gl(0), gl(1), gl(2),
            pl.BlockSpec(w_branch_bf16.shape, lambda i: (0, 0, 0)),
            pl.BlockSpec(w_out_bf16.shape, lambda i: (0, 0)),
            xs,
            pl.BlockSpec((1, 1, d), lambda i: (i // tiles_per_seq, 0, 0)),
            pl.BlockSpec((1, d), lambda i: (0, 0)),
        ],
        out_specs=xs,
        compiler_params=_params("arbitrary"),
        name="merge_out",
    )(o_pool, o_ret, o_rwkv, proj, proj, proj, w_branch_bf16, w_out_bf16, x2, gate,
      final_g.reshape(1, d))


def _permute_w_in(w_in):
    w = BRANCH_W
    rw = 6 * w
    r = w_in[..., rw:rw + w]
    wl = w_in[..., rw + w:rw + w + RWKV_LORA]
    k = w_in[..., rw + w + RWKV_LORA:rw + 2 * w + RWKV_LORA]
    v = w_in[..., rw + 2 * w + RWKV_LORA:rw + 3 * w + RWKV_LORA]
    al = w_in[..., rw + 3 * w + RWKV_LORA:rw + 3 * w + 2 * RWKV_LORA]
    rest = w_in[..., rw + 3 * w + 2 * RWKV_LORA:]
    pad = jnp.zeros(w_in.shape[:-1] + (D_IN_PAD - D_IN,), w_in.dtype)
    return jnp.concatenate([w_in[..., :rw], r, k, v, rest, wl, al, pad], axis=-1)


def kernel(x, c, positions, norm_g, w_ada, b_ada, w_in, pool_w, pool_scale, ret_norm_g,
           rwkv_shift_mu, rwkv_w0, rwkv_w2, rwkv_a0, rwkv_a2, rwkv_k_k, rwkv_k_a, rwkv_r_k,
           rwkv_ln_w, rwkv_ln_b, w_branch, w_out, final_g):
    batch, seq, d = x.shape
    depth = w_in.shape[0]
    ntok = batch * seq
    w = BRANCH_W
    assert d == D_MODEL and w_in.shape[-1] == D_IN
    assert seq % IN_TM == 0 and seq % POOL_TT == 0 and seq % (RET_GROUP * RET_CHUNK) == 0
    assert seq % (RWKV_GROUP * RWKV_CHUNK) == 0 and seq % MERGE_TM == 0

    mod_rows = 8
    c_pad = jnp.zeros((mod_rows, d), F32).at[:batch].set(c)
    mod = _modulation(c_pad, w_ada, b_ada)[:, :batch]
    cosf, sinf = _rope_tables(positions.reshape(ntok, 1))

    w_in_p = _permute_w_in(w_in.astype(BF16))
    pool_w_b = pool_w.astype(BF16)
    w_branch_b = w_branch.astype(BF16)
    w_out_b = w_out.astype(BF16)
    lora_pad = jnp.zeros((depth, RWKV_LORA, w), F32)
    w2_pad = jnp.concatenate([rwkv_w2, lora_pad], axis=1)
    a2_pad = jnp.concatenate([lora_pad, rwkv_a2], axis=1)
    mu = rwkv_shift_mu
    mu_lo = jnp.concatenate([mu[:, w:w + RWKV_LORA], mu[:, 3 * w + RWKV_LORA:]], axis=1)

    x2 = x.reshape(ntok, d)
    for l in range(depth):
        shift = mod[l, :, 0:d].reshape(batch, 1, d)
        scale = mod[l, :, d:2 * d].reshape(batch, 1, d)
        gate = mod[l, :, 2 * d:].reshape(batch, 1, d)
        proj = _in_projection(x2, norm_g[l], shift, scale, w_in_p[l], seq)
        o_pool = _pool_mixer(proj, pool_w_b[l], pool_scale[l], batch, seq)
        o_ret = _retention(proj, cosf, sinf, ret_norm_g[l], batch, seq)
        prm = dict(mu_r=mu[l, 0:w], mu_k=mu[l, w + RWKV_LORA:2 * w + RWKV_LORA],
                   mu_v=mu[l, 2 * w + RWKV_LORA:3 * w + RWKV_LORA], mu_lo=mu_lo[l],
                   w0=rwkv_w0[l], w2=w2_pad[l], a0=rwkv_a0[l], a2=a2_pad[l],
                   k_k=rwkv_k_k[l], k_a=rwkv_k_a[l], r_k=rwkv_r_k[l],
                   ln_w=rwkv_ln_w[l], ln_b=rwkv_ln_b[l])
        o_rwkv = _rwkv_mixer(proj, prm, batch, seq)
        x2 = _merge(o_pool, o_ret, o_rwkv, proj, w_branch_b[l], w_out_b[l], x2, gate, final_g,
                    seq, final_norm=(l == depth - 1))
    return x2.reshape(batch, seq, d)
```

```python
import functools
import math

import jax
import jax.numpy as jnp
from jax import lax
from jax.experimental import pallas as pl
from jax.experimental.pallas import tpu as pltpu

F32 = jnp.float32
BF16 = jnp.bfloat16

D_MODEL = 1024
BRANCH_W = D_MODEL // 2
N_BRANCHES = 3
POOL_WINDOWS = (2, 4, 8, 16)
POOL_GROUP_W = BRANCH_W // len(POOL_WINDOWS)
POOL_HALO = 16
RET_HEADS = 4
RET_HEAD_DIM = BRANCH_W // RET_HEADS
RET_CHUNK = 128
ROPE_BASE = 10000.0
RWKV_HEAD_DIM = 64
RWKV_HEADS = BRANCH_W // RWKV_HEAD_DIM
RWKV_LORA = 64
RWKV_CHUNK = 64
RWKV_GROUP = 4
RWKV_PAIR_W = 2 * RWKV_HEAD_DIM
RWKV_PAIRS = RWKV_HEADS // 2
NORM_EPS = 1e-6
RET_NORM_EPS = 1e-5
RWKV_NORM_EPS = 64e-5

COL_POOL_U, COL_POOL_G = 0, 1
COL_RET_Q, COL_RET_K, COL_RET_V, COL_RET_G = 2, 3, 4, 5
COL_RW_R, COL_RW_K, COL_RW_V, COL_RW_G = 6, 7, 8, 9
COL_GATES_1024 = 5
COL_LORA_128 = 64
D_IN = 10 * BRANCH_W + N_BRANCHES * D_MODEL + 2 * RWKV_LORA
MXU_WIDTH = 256
D_IN_PAD = -(-D_IN // MXU_WIDTH) * MXU_WIDTH

VMEM_LIMIT_BYTES = 56 * 1024 * 1024

IN_TM, IN_TN = 1024, D_IN_PAD // 3
IN_SUB_TN = 3 * MXU_WIDTH
W_PREP_ROWS = 256
POOL_TT = 512
RET_GROUP = 4
MERGE_TM = 512
ROPE_TT = 1024


def _dot(a, b):
    return jnp.dot(a.astype(BF16), b.astype(BF16), preferred_element_type=F32)


def _bmm(a, b):
    return lax.dot_general(a.astype(BF16), b.astype(BF16), (((2,), (1,)), ((0,), (0,))),
                           preferred_element_type=F32)


def _bmm_nt(a, b):
    return lax.dot_general(a.astype(BF16), b.astype(BF16), (((2,), (2,)), ((0,), (0,))),
                           preferred_element_type=F32)


def _split3(x):
    hi = x.astype(BF16)
    r1 = x - hi.astype(F32)
    mid = r1.astype(BF16)
    lo = (r1 - mid.astype(F32)).astype(BF16)
    return hi, mid, lo


def _dot_hilo(a, b):
    a_hi = a.astype(BF16)
    a_lo = (a - a_hi.astype(F32)).astype(BF16)
    b_hi = b.astype(BF16)
    b_lo = (b - b_hi.astype(F32)).astype(BF16)
    acc = jnp.dot(a_lo, b_hi, preferred_element_type=F32)
    acc = acc + jnp.dot(a_hi, b_lo, preferred_element_type=F32)
    return acc + jnp.dot(a_hi, b_hi, preferred_element_type=F32)


def _silu(x):
    return x * jax.nn.sigmoid(x)


def _params(*sem):
    return pltpu.CompilerParams(dimension_semantics=sem, vmem_limit_bytes=VMEM_LIMIT_BYTES)


def _mod_kernel(c_ref, w_ref, b_ref, o_ref):
    o_ref[0] = _dot(_silu(c_ref[...]), w_ref[0]) + b_ref[0]


def _modulation(c_pad, w_ada, b_ada):
    depth, d, d3 = w_ada.shape
    rows = c_pad.shape[0]
    return pl.pallas_call(
        _mod_kernel,
        out_shape=jax.ShapeDtypeStruct((depth, rows, d3), F32),
        grid=(depth, d3 // d),
        in_specs=[
            pl.BlockSpec((rows, d), lambda l, j: (0, 0)),
            pl.BlockSpec((1, d, d), lambda l, j: (l, 0, j)),
            pl.BlockSpec((1, 1, d), lambda l, j: (l, 0, j)),
        ],
        out_specs=pl.BlockSpec((1, rows, d), lambda l, j: (l, 0, j)),
        compiler_params=_params("arbitrary", "arbitrary"),
        name="adaln_mod",
    )(c_pad, w_ada, b_ada.reshape(depth, 1, d3))


def _rope_kernel(pos_ref, freq_ref, sign_ref, cos_ref, sin_ref):
    ang = pos_ref[...].astype(F32) * freq_ref[...]
    cos_ref[...] = jnp.cos(ang)
    sin_ref[...] = jnp.sin(ang) * sign_ref[...]


def _rope_tables(pos_col):
    ntok = pos_col.shape[0]
    half = RET_HEAD_DIM // 2
    freqs = ROPE_BASE ** (-jnp.arange(half, dtype=F32) / half)
    freq2 = jnp.concatenate([freqs, freqs]).reshape(1, RET_HEAD_DIM)
    sign = jnp.concatenate([-jnp.ones((half,), F32), jnp.ones((half,), F32)]).reshape(1, RET_HEAD_DIM)
    vec = pl.BlockSpec((1, RET_HEAD_DIM), lambda i: (0, 0))
    tab = pl.BlockSpec((ROPE_TT, RET_HEAD_DIM), lambda i: (i, 0))
    return pl.pallas_call(
        _rope_kernel,
        out_shape=(jax.ShapeDtypeStruct((ntok, RET_HEAD_DIM), F32),) * 2,
        grid=(ntok // ROPE_TT,),
        in_specs=[pl.BlockSpec((ROPE_TT, 1), lambda i: (i, 0)), vec, vec],
        out_specs=(tab, tab),
        compiler_params=_params("arbitrary"),
        name="rope_tables",
    )(pos_col, freq2, sign)


def _inproj_kernel(x_ref, g_ref, shift_ref, scale_ref, w_ref, o_ref, h_ref):
    @pl.when(pl.program_id(1) == 0)
    def _():
        x = x_ref[...]
        y = x * lax.rsqrt(jnp.mean(x * x, axis=-1, keepdims=True) + NORM_EPS) * g_ref[...]
        h_ref[...] = (y * (1.0 + scale_ref[0]) + shift_ref[0]).astype(BF16)

    h = h_ref[...]
    for c0 in range(0, IN_TN, IN_SUB_TN):
        c1 = min(c0 + IN_SUB_TN, IN_TN)
        o_ref[:, c0:c1] = jnp.dot(h, w_ref[0, :, c0:c1], preferred_element_type=F32).astype(BF16)


def _in_projection(x2, norm_g, shift, scale, w_in_bf16, layer, seq):
    ntok, d = x2.shape
    tiles_per_seq = seq // IN_TM
    vec = pl.BlockSpec((1, 1, d), lambda i, j: (i // tiles_per_seq, 0, 0))
    return pl.pallas_call(
        _inproj_kernel,
        out_shape=jax.ShapeDtypeStruct((ntok, D_IN_PAD), BF16),
        grid=(ntok // IN_TM, D_IN_PAD // IN_TN),
        in_specs=[
            pl.BlockSpec((IN_TM, d), lambda i, j: (i, 0)),
            pl.BlockSpec((1, d), lambda i, j: (0, 0)),
            vec, vec,
            pl.BlockSpec((1, d, IN_TN), lambda i, j: (layer, 0, j)),
        ],
        out_specs=pl.BlockSpec((IN_TM, IN_TN), lambda i, j: (i, j)),
        scratch_shapes=[pltpu.VMEM((IN_TM, d), BF16)],
        compiler_params=_params("arbitrary", "arbitrary"),
        name="in_projection",
    )(x2, norm_g.reshape(1, d), shift, scale, w_in_bf16)


def _pool_kernel(u_ref, g_ref, w_ref, sc_ref, o_ref, ext_ref):
    t = pl.program_id(1)
    tt = u_ref.shape[0]

    @pl.when(t == 0)
    def _():
        ext_ref[0:POOL_HALO, :] = jnp.zeros((POOL_HALO, BRANCH_W), F32)

    @pl.when(t > 0)
    def _():
        ext_ref[0:POOL_HALO, :] = ext_ref[tt:tt + POOL_HALO, :]

    u = u_ref[...].astype(F32)
    ext_ref[POOL_HALO:POOL_HALO + tt, :] = u
    pos = lax.broadcasted_iota(jnp.int32, (tt, POOL_GROUP_W), 0) + t * tt
    outs = []
    for gi, win in enumerate(POOL_WINDOWS):
        c0 = gi * POOL_GROUP_W
        acc = u[:, c0:c0 + POOL_GROUP_W]
        for k in range(1, win):
            acc = acc + ext_ref[POOL_HALO - k:POOL_HALO - k + tt, c0:c0 + POOL_GROUP_W]
        count = jnp.minimum(pos + 1, win).astype(F32)
        d = acc / count - u[:, c0:c0 + POOL_GROUP_W]
        outs.append(_dot(d, w_ref[gi]))
    y = jnp.concatenate(outs, axis=1) * sc_ref[...]
    o_ref[...] = (y * _silu(g_ref[...].astype(F32))).astype(BF16)


def _pool_mixer(proj, pool_w_bf16, pool_scale, batch, seq):
    ntok = proj.shape[0]
    tiles = seq // POOL_TT
    blk = lambda col: pl.BlockSpec((POOL_TT, BRANCH_W), lambda b, t: (b * tiles + t, col))
    return pl.pallas_call(
        _pool_kernel,
        out_shape=jax.ShapeDtypeStruct((ntok, BRANCH_W), BF16),
        grid=(batch, tiles),
        in_specs=[
            blk(COL_POOL_U), blk(COL_POOL_G),
            pl.BlockSpec(pool_w_bf16.shape, lambda b, t: (0, 0, 0)),
            pl.BlockSpec((1, BRANCH_W), lambda b, t: (0, 0)),
        ],
        out_specs=blk(0),
        scratch_shapes=[pltpu.VMEM((POOL_TT + POOL_HALO, BRANCH_W), F32)],
        compiler_params=_params("arbitrary", "arbitrary"),
        name="pool_mixer",
    )(proj, proj, pool_w_bf16, pool_scale.reshape(1, BRANCH_W))


def _ret_log_gamma():
    return [math.log1p(-(2.0 ** (-5.0 - h))) for h in range(RET_HEADS)]


def _ret_kernel(q_ref, k_ref, v_ref, g_ref, cos_ref, sin_ref, mask_ref, qd_ref, kt_ref, cd_ref,
                ng_ref, o_ref, r_ref):
    @pl.when(pl.program_id(1) == 0)
    def _():
        r_ref[...] = jnp.zeros(r_ref.shape, F32)

    dh = RET_HEAD_DIM
    c = RET_CHUNK
    nh = RET_HEADS
    blocks = [(g, h) for g in range(RET_GROUP) for h in range(nh)]
    cosf = cos_ref[...]
    sinf = sin_ref[...]

    def heads(ref, rotate):
        x = ref[...].astype(F32)
        parts = []
        for g, h in blocks:
            xs = x[g * c:(g + 1) * c, h * dh:(h + 1) * dh]
            if rotate:
                rows = slice(g * c, (g + 1) * c)
                xs = xs * cosf[rows] + pltpu.roll(xs, dh // 2, 1) * sinf[rows]
            parts.append(xs)
        return jnp.stack(parts, axis=0)

    qr = heads(q_ref, True)
    kr = heads(k_ref, True) * (dh ** -0.5)
    v = heads(v_ref, False)
    o = _bmm(_bmm_nt(qr, kr) * mask_ref[...], v)
    kv = _bmm(jnp.swapaxes(kr * kt_ref[...], 1, 2), v)
    state = r_ref[...]
    before = []
    for g in range(RET_GROUP):
        before.append(state)
        state = cd_ref[...] * state + kv[g * nh:(g + 1) * nh]
    r_ref[...] = state
    o = o + _bmm(qr * qd_ref[...], jnp.concatenate(before, axis=0))
    xc = o - jnp.mean(o, axis=-1, keepdims=True)
    var = jnp.mean(xc * xc, axis=-1, keepdims=True)
    on = xc * lax.rsqrt(var + RET_NORM_EPS)
    rows = [jnp.concatenate([on[g * nh + h] for h in range(nh)], axis=1) for g in range(RET_GROUP)]
    o = jnp.concatenate(rows, axis=0) * ng_ref[...]
    o_ref[...] = (o * _silu(g_ref[...].astype(F32))).astype(BF16)


def _retention(proj, cosf, sinf, ret_norm_g, batch, seq):
    ntok = proj.shape[0]
    c = RET_CHUNK
    rows = RET_GROUP * c
    steps = seq // rows
    lg = jnp.asarray(_ret_log_gamma(), F32)
    i = jnp.arange(c, dtype=F32)
    diff = i[:, None] - i[None, :]
    mask = jnp.where(diff >= 0, jnp.exp(lg[:, None, None] * jnp.maximum(diff, 0.0)), 0.0)
    ones = jnp.ones((1, 1, RET_HEAD_DIM), F32)
    q_dec = jnp.exp(lg[:, None] * (i + 1.0))[:, :, None] * ones
    k_tail = jnp.exp(lg[:, None] * (c - 1 - i))[:, :, None] * ones
    chunk_decay = jnp.exp(lg * c)[:, None, None] * ones
    per_group = lambda a: jnp.tile(a, (RET_GROUP, 1, 1))
    mask, q_dec, k_tail = per_group(mask), per_group(q_dec), per_group(k_tail)
    blk = lambda col: pl.BlockSpec((rows, BRANCH_W), lambda b, n: (b * steps + n, col))
    tab = pl.BlockSpec((rows, RET_HEAD_DIM), lambda b, n: (b * steps + n, 0))
    const3 = lambda a: pl.BlockSpec(a.shape, lambda b, n: (0, 0, 0))
    return pl.pallas_call(
        _ret_kernel,
        out_shape=jax.ShapeDtypeStruct((ntok, BRANCH_W), BF16),
        grid=(batch, steps),
        in_specs=[
            blk(COL_RET_Q), blk(COL_RET_K), blk(COL_RET_V), blk(COL_RET_G),
            tab, tab, const3(mask), const3(q_dec), const3(k_tail), const3(chunk_decay),
            pl.BlockSpec((1, BRANCH_W), lambda b, n: (0, 0)),
        ],
        out_specs=blk(0),
        scratch_shapes=[pltpu.VMEM((RET_HEADS, RET_HEAD_DIM, RET_HEAD_DIM), F32)],
        compiler_params=_params("arbitrary", "arbitrary"),
        name="retention",
    )(proj, proj, proj, proj, cosf, sinf, mask, q_dec, k_tail, chunk_decay,
      ret_norm_g.reshape(1, BRANCH_W))


def _inv_unit_lower(m, row, col):
    same = lambda n: ((row // n) == (col // n))[None]
    eye = (row == col).astype(F32)[None]
    md = jnp.where(same(8), m, 0.0)
    m2 = _bmm(md, md)
    t = _bmm(_bmm(eye + md, eye + m2), eye + _bmm(m2, m2))
    n = 8
    while n < RWKV_CHUNK:
        off = jnp.where(same(2 * n) & jnp.logical_not(same(n)), m, 0.0)
        t = t + _bmm(t, _bmm(off, t))
        n *= 2
    return t


def _rwkv_kernel(pr_ref, pk_ref, pv_ref, pg_ref, plo_ref,
                 mur_ref, muk_ref, muv_ref, mulo_ref,
                 w0_ref, w2_ref, a0_ref, a2_ref, kk_ref, ka_ref, rk_ref, lnw_ref, lnb_ref,
                 seg_ref, o_ref,
                 z_ref, prev_r, prev_k, prev_v, prev_lo):
    L = RWKV_CHUNK
    G = RWKV_GROUP
    GL = G * L
    hd = RWKV_HEAD_DIM
    pw = RWKV_PAIR_W
    P = RWKV_PAIRS

    @pl.when(pl.program_id(1) == 0)
    def _():
        z_ref[...] = jnp.zeros(z_ref.shape, F32)
        prev_r[...] = jnp.zeros(prev_r.shape, F32)
        prev_k[...] = jnp.zeros(prev_k.shape, F32)
        prev_v[...] = jnp.zeros(prev_v.shape, F32)
        prev_lo[...] = jnp.zeros(prev_lo.shape, F32)

    def shifted(p_ref, prev_ref, mu_ref):
        p = p_ref[...].astype(F32)
        first = lax.broadcasted_iota(jnp.int32, p.shape, 0) == 0
        p_prev = jnp.where(first, prev_ref[...], pltpu.roll(p, 1, 0))
        prev_ref[...] = p[GL - 1:GL, :]
        return p + (p_prev - p) * mu_ref[...]

    r = shifted(pr_ref, prev_r, mur_ref)
    k = shifted(pk_ref, prev_k, muk_ref)
    v = shifted(pv_ref, prev_v, muv_ref)
    lo = shifted(plo_ref, prev_lo, mulo_ref)

    seg = seg_ref[...]
    slabs = BRANCH_W // pw

    def seg_sum(x):
        n = x.shape[0]
        xs = jnp.concatenate([x[:, s * pw:(s + 1) * pw] for s in range(slabs)], axis=0)
        hi = xs.astype(BF16)
        lo_part = (xs - hi.astype(F32)).astype(BF16)
        s2 = (jnp.dot(lo_part, seg, preferred_element_type=F32)
              + jnp.dot(hi, seg, preferred_element_type=F32))
        return jnp.concatenate([s2[s * n:(s + 1) * n] for s in range(slabs)], axis=1)

    zw = w0_ref[...] + _dot_hilo(jnp.tanh(lo), w2_ref[...])
    neg = -zw
    softplus = jnp.maximum(neg, 0.0) + jnp.log(1.0 + jnp.exp(-jnp.abs(neg)))
    logw = -jnp.exp(-softplus - 0.5)
    lr = jax.nn.sigmoid(a0_ref[...] + _dot_hilo(lo, a2_ref[...]))
    kk = k * kk_ref[...]
    kk = kk * jnp.minimum(lax.rsqrt(seg_sum(kk * kk)), 1e12)
    k = k * (1.0 + (lr - 1.0) * ka_ref[...])
    a_n = -kk
    b_n = kk * lr

    ti = lax.broadcasted_iota(jnp.int32, (G, L, L), 1)
    tj = lax.broadcasted_iota(jnp.int32, (G, L, L), 2)
    tri = (ti >= tj).astype(BF16)
    hi, mid, low = _split3(jnp.stack([logw[g * L:(g + 1) * L] for g in range(G)], axis=0))
    tri_dot = lambda piece: lax.dot_general(tri, piece, (((2,), (1,)), ((0,), (0,))),
                                            preferred_element_type=F32)
    c3 = tri_dot(low) + tri_dot(mid) + tri_dot(hi)
    c = jnp.concatenate([c3[g] for g in range(G)], axis=0)
    c_last = [c[(g + 1) * L - 1:(g + 1) * L, :] for g in range(G)]
    c_end = jnp.concatenate([jnp.broadcast_to(cl, (L, BRANCH_W)) for cl in c_last], axis=0)
    e_c = jnp.exp(c)
    e_nc = jnp.exp(-c)
    e_tail = jnp.exp(c_end - c)
    a_t = a_n * jnp.exp(c - logw)
    b_t = b_n * e_nc
    k_t = k * e_nc
    r_t = r * e_c
    b_h = b_n * e_tail
    k_h = k * e_tail

    lane = lax.broadcasted_iota(jnp.int32, (pw, pw), 1)
    row = lax.broadcasted_iota(jnp.int32, (pw, pw), 0)
    head_mask = ((lane < hd) == (row < L)).astype(F32)
    same_head = (row // L) == (lane // L)
    strict = (same_head & ((row % L) > (lane % L)))[None]
    incl = (same_head & ((row % L) >= (lane % L)))[None]
    eye = (row == lane)[None]

    def stack(x, masked=True):
        parts = []
        for g in range(G):
            for p in range(P):
                xs = x[g * L:(g + 1) * L, p * pw:(p + 1) * pw]
                xs = jnp.concatenate([xs, xs], axis=0)
                parts.append(xs * head_mask if masked else xs)
        return jnp.stack(parts, axis=0)

    a_s, r_s = stack(a_t), stack(r_t)
    b_s, k_s = stack(b_t, masked=False), stack(k_t, masked=False)
    bh_s, kh_s, v_s = stack(b_h), stack(k_h), stack(v)
    w_end = jnp.stack([jnp.exp(c_last[g][:, p * pw:(p + 1) * pw])
                       for g in range(G) for p in range(P)], axis=0)

    mm = _bmm_nt(jnp.concatenate([a_s, r_s], axis=1), jnp.concatenate([b_s, k_s], axis=1))
    m_ab = jnp.where(strict, mm[:, 0:pw, 0:pw], 0.0)
    m_ak = jnp.where(strict, mm[:, 0:pw, pw:2 * pw], 0.0)
    a_rb = jnp.where(incl, mm[:, pw:2 * pw, 0:pw], 0.0)
    a_rk = jnp.where(incl, mm[:, pw:2 * pw, pw:2 * pw], 0.0)
    t_inv = _inv_unit_lower(m_ab, row, lane)
    x = _bmm(t_inv, jnp.concatenate([a_s, _bmm(m_ak, v_s)], axis=2))
    via_u = _bmm(jnp.concatenate([jnp.swapaxes(bh_s, 1, 2), a_rb], axis=1), x)
    via_v = _bmm(jnp.concatenate([jnp.swapaxes(kh_s, 1, 2), a_rk], axis=1), v_s)
    phi = via_u[:, 0:pw, 0:pw] + jnp.where(eye, w_end, 0.0)
    gg = via_u[:, 0:pw, pw:2 * pw] + via_v[:, 0:pw]
    q = r_s + via_u[:, pw:2 * pw, 0:pw]
    y0 = via_u[:, pw:2 * pw, pw:2 * pw] + via_v[:, pw:2 * pw]

    z = z_ref[...]
    rows = []
    for g in range(G):
        cs = slice(g * P, (g + 1) * P)
        y_s = _bmm(q[cs], z) + y0[cs]
        z = _bmm(phi[cs], z) + gg[cs]
        rows.append(jnp.concatenate([y_s[p, 0:L] + y_s[p, L:2 * L] for p in range(P)], axis=1))
    z_ref[...] = z

    y = jnp.concatenate(rows, axis=0)
    xc = y - seg_sum(y) * (1.0 / hd)
    var = seg_sum(xc * xc) * (1.0 / hd)
    y = xc * lax.rsqrt(var + RWKV_NORM_EPS) * lnw_ref[...] + lnb_ref[...]
    y = y + seg_sum(r * k * rk_ref[...]) * v
    o_ref[...] = (y * _silu(pg_ref[...].astype(F32))).astype(BF16)


def _rwkv_mixer(proj, prm, batch, seq):
    ntok = proj.shape[0]
    rows_per_step = RWKV_GROUP * RWKV_CHUNK
    steps = seq // rows_per_step
    w = BRANCH_W
    blk = lambda col: pl.BlockSpec((rows_per_step, w), lambda b, n: (b * steps + n, col))
    vec = lambda width: pl.BlockSpec((1, width), lambda b, n: (0, 0))
    mat = lambda a: pl.BlockSpec(a.shape, lambda b, n: (0, 0))
    head = jnp.arange(RWKV_PAIR_W) // RWKV_HEAD_DIM
    seg = (head[:, None] == head[None, :]).astype(BF16)
    row = lambda a: a.reshape(1, -1)
    return pl.pallas_call(
        _rwkv_kernel,
        out_shape=jax.ShapeDtypeStruct((ntok, w), BF16),
        grid=(batch, steps),
        in_specs=[
            blk(COL_RW_R), blk(COL_RW_K), blk(COL_RW_V), blk(COL_RW_G),
            pl.BlockSpec((rows_per_step, 2 * RWKV_LORA),
                         lambda b, n: (b * steps + n, COL_LORA_128)),
            vec(w), vec(w), vec(w), vec(2 * RWKV_LORA),
            vec(w), mat(prm["w2"]), vec(w), mat(prm["a2"]),
            vec(w), vec(w), vec(w), vec(w), vec(w),
            mat(seg),
        ],
        out_specs=blk(0),
        scratch_shapes=[
            pltpu.VMEM((RWKV_PAIRS, RWKV_PAIR_W, RWKV_PAIR_W), F32),
            pltpu.VMEM((1, w), F32), pltpu.VMEM((1, w), F32), pltpu.VMEM((1, w), F32),
            pltpu.VMEM((1, 2 * RWKV_LORA), F32),
        ],
        compiler_params=_params("arbitrary", "arbitrary"),
        name="rwkv7_mixer",
    )(proj, proj, proj, proj, proj,
      row(prm["mu_r"]), row(prm["mu_k"]), row(prm["mu_v"]), row(prm["mu_lo"]),
      row(prm["w0"]), prm["w2"], row(prm["a0"]), prm["a2"],
      row(prm["k_k"]), row(prm["k_a"]), row(prm["r_k"]), row(prm["ln_w"]), row(prm["ln_b"]),
      seg)


def _merge_kernel(op_ref, or_ref, ow_ref, g0_ref, g1_ref, g2_ref, wb_ref, wo_ref, x_ref,
                  gate_ref, fg_ref, o_ref, wb_bf16, wo_bf16, *, final_norm):
    @pl.when(pl.program_id(0) == 0)
    def _():
        wb_bf16[...] = wb_ref[0].astype(BF16)
        wo_bf16[...] = wo_ref[0].astype(BF16)

    merged = None
    for n, (br_ref, gl_ref) in enumerate(((op_ref, g0_ref), (or_ref, g1_ref), (ow_ref, g2_ref))):
        term = jax.nn.sigmoid(gl_ref[...].astype(F32)) * _dot(br_ref[...], wb_bf16[n])
        merged = term if merged is None else merged + term
    out = x_ref[...] + gate_ref[0] * _dot(merged, wo_bf16[...])
    if final_norm:
        out = out * lax.rsqrt(jnp.mean(out * out, axis=-1, keepdims=True) + NORM_EPS) * fg_ref[...]
    o_ref[...] = out


def _merge(o_pool, o_ret, o_rwkv, proj, w_branch, w_out, layer, x2, gate, final_g, seq,
           final_norm):
    ntok, d = x2.shape
    tm = MERGE_TM
    tiles_per_seq = seq // tm
    br = pl.BlockSpec((tm, BRANCH_W), lambda i: (i, 0))
    gl = lambda n: pl.BlockSpec((tm, d), lambda i: (i, COL_GATES_1024 + n))
    xs = pl.BlockSpec((tm, d), lambda i: (i, 0))
    return pl.pallas_call(
        functools.partial(_merge_kernel, final_norm=final_norm),
        out_shape=jax.ShapeDtypeStruct((ntok, d), F32),
        grid=(ntok // tm,),
        in_specs=[
            br, br, br, gl(0), gl(1), gl(2),
            pl.BlockSpec((1,) + w_branch.shape[1:], lambda i: (layer, 0, 0, 0)),
            pl.BlockSpec((1,) + w_out.shape[1:], lambda i: (layer, 0, 0)),
            xs,
            pl.BlockSpec((1, 1, d), lambda i: (i // tiles_per_seq, 0, 0)),
            pl.BlockSpec((1, d), lambda i: (0, 0)),
        ],
        out_specs=xs,
        scratch_shapes=[pltpu.VMEM(w_branch.shape[1:], BF16), pltpu.VMEM(w_out.shape[1:], BF16)],
        compiler_params=_params("arbitrary"),
        name="merge_out",
    )(o_pool, o_ret, o_rwkv, proj, proj, proj, w_branch, w_out, x2, gate,
      final_g.reshape(1, d))


def _wprep_kernel(w_ref, o_ref):
    w = BRANCH_W
    lo = RWKV_LORA
    r_end = 7 * w
    rest_src = r_end + 2 * w + 2 * lo
    rest_dst = r_end + 2 * w
    rows = w_ref.shape[1]
    o_ref[0, :, 0:r_end] = w_ref[0, :, 0:r_end].astype(BF16)
    mid = w_ref[0, :, r_end:rest_src]
    o_ref[0, :, r_end:r_end + w] = mid[:, lo:lo + w].astype(BF16)
    o_ref[0, :, r_end + w:rest_dst] = mid[:, lo + w:lo + 2 * w].astype(BF16)
    o_ref[0, :, rest_dst:rest_dst + (D_IN - rest_src)] = w_ref[0, :, rest_src:D_IN].astype(BF16)
    tail = jnp.concatenate([mid[:, 0:lo], mid[:, lo + 2 * w:2 * lo + 2 * w],
                            jnp.zeros((rows, D_IN_PAD - D_IN), F32)], axis=1)
    o_ref[0, :, D_IN - 2 * lo:D_IN_PAD] = tail.astype(BF16)


def _prepare_w_in(w_in):
    depth, d, d_in = w_in.shape
    return pl.pallas_call(
        _wprep_kernel,
        out_shape=jax.ShapeDtypeStruct((depth, d, D_IN_PAD), BF16),
        grid=(depth, d // W_PREP_ROWS),
        in_specs=[pl.BlockSpec((1, W_PREP_ROWS, d_in), lambda l, i: (l, i, 0))],
        out_specs=pl.BlockSpec((1, W_PREP_ROWS, D_IN_PAD), lambda l, i: (l, i, 0)),
        compiler_params=_params("arbitrary", "arbitrary"),
        name="w_in_prep",
    )(w_in)


def kernel(x, c, positions, norm_g, w_ada, b_ada, w_in, pool_w, pool_scale, ret_norm_g,
           rwkv_shift_mu, rwkv_w0, rwkv_w2, rwkv_a0, rwkv_a2, rwkv_k_k, rwkv_k_a, rwkv_r_k,
           rwkv_ln_w, rwkv_ln_b, w_branch, w_out, final_g):
    batch, seq, d = x.shape
    depth = w_in.shape[0]
    ntok = batch * seq
    w = BRANCH_W
    assert d == D_MODEL and w_in.shape[-1] == D_IN
    assert seq % IN_TM == 0 and seq % POOL_TT == 0 and seq % (RET_GROUP * RET_CHUNK) == 0
    assert seq % (RWKV_GROUP * RWKV_CHUNK) == 0 and seq % MERGE_TM == 0

    mod_rows = 8
    c_pad = jnp.zeros((mod_rows, d), F32).at[:batch].set(c)
    mod = _modulation(c_pad, w_ada, b_ada)[:, :batch]
    cosf, sinf = _rope_tables(positions.reshape(ntok, 1))

    w_in_p = _prepare_w_in(w_in)
    pool_w_b = pool_w.astype(BF16)
    lora_pad = jnp.zeros((depth, RWKV_LORA, w), F32)
    w2_pad = jnp.concatenate([rwkv_w2, lora_pad], axis=1)
    a2_pad = jnp.concatenate([lora_pad, rwkv_a2], axis=1)
    mu = rwkv_shift_mu
    mu_lo = jnp.concatenate([mu[:, w:w + RWKV_LORA], mu[:, 3 * w + RWKV_LORA:]], axis=1)

    x2 = x.reshape(ntok, d)
    for l in range(depth):
        shift = mod[l, :, 0:d].reshape(batch, 1, d)
        scale = mod[l, :, d:2 * d].reshape(batch, 1, d)
        gate = mod[l, :, 2 * d:].reshape(batch, 1, d)
        proj = _in_projection(x2, norm_g[l], shift, scale, w_in_p, l, seq)
        o_pool = _pool_mixer(proj, pool_w_b[l], pool_scale[l], batch, seq)
        o_ret = _retention(proj, cosf, sinf, ret_norm_g[l], batch, seq)
        prm = dict(mu_r=mu[l, 0:w], mu_k=mu[l, w + RWKV_LORA:2 * w + RWKV_LORA],
                   mu_v=mu[l, 2 * w + RWKV_LORA:3 * w + RWKV_LORA], mu_lo=mu_lo[l],
                   w0=rwkv_w0[l], w2=w2_pad[l], a0=rwkv_a0[l], a2=a2_pad[l],
                   k_k=rwkv_k_k[l], k_a=rwkv_k_a[l], r_k=rwkv_r_k[l],
                   ln_w=rwkv_ln_w[l], ln_b=rwkv_ln_b[l])
        o_rwkv = _rwkv_mixer(proj, prm, batch, seq)
        x2 = _merge(o_pool, o_ret, o_rwkv, proj, w_branch, w_out, l, x2, gate, final_g,
                    seq, final_norm=(l == depth - 1))
    return x2.reshape(batch, seq, d)
```

```python
import functools
import math

import jax
import jax.numpy as jnp
from jax import lax
from jax.experimental import pallas as pl
from jax.experimental.pallas import tpu as pltpu

F32 = jnp.float32
BF16 = jnp.bfloat16

D_MODEL = 1024
BRANCH_W = D_MODEL // 2
N_BRANCHES = 3
POOL_WINDOWS = (2, 4, 8, 16)
POOL_GROUP_W = BRANCH_W // len(POOL_WINDOWS)
POOL_HALO = 16
RET_HEADS = 4
RET_HEAD_DIM = BRANCH_W // RET_HEADS
RET_CHUNK = 128
ROPE_BASE = 10000.0
RWKV_HEAD_DIM = 64
RWKV_HEADS = BRANCH_W // RWKV_HEAD_DIM
RWKV_LORA = 64
RWKV_CHUNK = 64
RWKV_GROUP = 8
RWKV_PAIR_W = 2 * RWKV_HEAD_DIM
RWKV_PAIRS = RWKV_HEADS // 2
NORM_EPS = 1e-6
RET_NORM_EPS = 1e-5
RWKV_NORM_EPS = 64e-5

COL_POOL_U, COL_POOL_G = 0, 1
COL_RET_Q, COL_RET_K, COL_RET_V, COL_RET_G = 2, 3, 4, 5
COL_RW_R, COL_RW_K, COL_RW_V, COL_RW_G = 6, 7, 8, 9
COL_GATES_1024 = 5
COL_LORA_128 = 64
D_IN = 10 * BRANCH_W + N_BRANCHES * D_MODEL + 2 * RWKV_LORA
MXU_WIDTH = 256
D_IN_PAD = -(-D_IN // MXU_WIDTH) * MXU_WIDTH

VMEM_LIMIT_BYTES = 56 * 1024 * 1024

IN_TM, IN_TN = 1024, D_IN_PAD // 3
IN_SUB_TN = 3 * MXU_WIDTH
W_PREP_ROWS = 256
POOL_TT = 512
RET_GROUP = 4
MERGE_TM = 512
ROPE_TT = 1024


def _dot(a, b):
    return jnp.dot(a.astype(BF16), b.astype(BF16), preferred_element_type=F32)


def _bmm(a, b):
    return lax.dot_general(a.astype(BF16), b.astype(BF16), (((2,), (1,)), ((0,), (0,))),
                           preferred_element_type=F32)


def _bmm_nt(a, b):
    return lax.dot_general(a.astype(BF16), b.astype(BF16), (((2,), (2,)), ((0,), (0,))),
                           preferred_element_type=F32)


def _split3(x):
    hi = x.astype(BF16)
    r1 = x - hi.astype(F32)
    mid = r1.astype(BF16)
    lo = (r1 - mid.astype(F32)).astype(BF16)
    return hi, mid, lo


def _dot_hilo(a, b):
    a_hi = a.astype(BF16)
    a_lo = (a - a_hi.astype(F32)).astype(BF16)
    b_hi = b.astype(BF16)
    b_lo = (b - b_hi.astype(F32)).astype(BF16)
    acc = jnp.dot(a_lo, b_hi, preferred_element_type=F32)
    acc = acc + jnp.dot(a_hi, b_lo, preferred_element_type=F32)
    return acc + jnp.dot(a_hi, b_hi, preferred_element_type=F32)


def _silu(x):
    return x * jax.nn.sigmoid(x)


def _params(*sem):
    return pltpu.CompilerParams(dimension_semantics=sem, vmem_limit_bytes=VMEM_LIMIT_BYTES)


def _mod_kernel(c_ref, w_ref, b_ref, o_ref):
    o_ref[0] = _dot(_silu(c_ref[...]), w_ref[0]) + b_ref[0]


def _modulation(c_pad, w_ada, b_ada):
    depth, d, d3 = w_ada.shape
    rows = c_pad.shape[0]
    return pl.pallas_call(
        _mod_kernel,
        out_shape=jax.ShapeDtypeStruct((depth, rows, d3), F32),
        grid=(depth, d3 // d),
        in_specs=[
            pl.BlockSpec((rows, d), lambda l, j: (0, 0)),
            pl.BlockSpec((1, d, d), lambda l, j: (l, 0, j)),
            pl.BlockSpec((1, 1, d), lambda l, j: (l, 0, j)),
        ],
        out_specs=pl.BlockSpec((1, rows, d), lambda l, j: (l, 0, j)),
        compiler_params=_params("arbitrary", "arbitrary"),
        name="adaln_mod",
    )(c_pad, w_ada, b_ada.reshape(depth, 1, d3))


def _rope_kernel(pos_ref, freq_ref, sign_ref, cos_ref, sin_ref):
    ang = pos_ref[...].astype(F32) * freq_ref[...]
    cos_ref[...] = jnp.cos(ang)
    sin_ref[...] = jnp.sin(ang) * sign_ref[...]


def _rope_tables(pos_col):
    ntok = pos_col.shape[0]
    half = RET_HEAD_DIM // 2
    freqs = ROPE_BASE ** (-jnp.arange(half, dtype=F32) / half)
    freq2 = jnp.concatenate([freqs, freqs]).reshape(1, RET_HEAD_DIM)
    sign = jnp.concatenate([-jnp.ones((half,), F32), jnp.ones((half,), F32)]).reshape(1, RET_HEAD_DIM)
    vec = pl.BlockSpec((1, RET_HEAD_DIM), lambda i: (0, 0))
    tab = pl.BlockSpec((ROPE_TT, RET_HEAD_DIM), lambda i: (i, 0))
    return pl.pallas_call(
        _rope_kernel,
        out_shape=(jax.ShapeDtypeStruct((ntok, RET_HEAD_DIM), F32),) * 2,
        grid=(ntok // ROPE_TT,),
        in_specs=[pl.BlockSpec((ROPE_TT, 1), lambda i: (i, 0)), vec, vec],
        out_specs=(tab, tab),
        compiler_params=_params("arbitrary"),
        name="rope_tables",
    )(pos_col, freq2, sign)


def _inproj_kernel(x_ref, g_ref, shift_ref, scale_ref, w_ref, o_ref, h_ref):
    @pl.when(pl.program_id(1) == 0)
    def _():
        x = x_ref[...]
        y = x * lax.rsqrt(jnp.mean(x * x, axis=-1, keepdims=True) + NORM_EPS) * g_ref[...]
        h_ref[...] = (y * (1.0 + scale_ref[0]) + shift_ref[0]).astype(BF16)

    h = h_ref[...]
    for c0 in range(0, IN_TN, IN_SUB_TN):
        c1 = min(c0 + IN_SUB_TN, IN_TN)
        o_ref[:, c0:c1] = jnp.dot(h, w_ref[0, :, c0:c1], preferred_element_type=F32).astype(BF16)


def _in_projection(x2, norm_g, shift, scale, w_in_bf16, layer, seq):
    ntok, d = x2.shape
    tiles_per_seq = seq // IN_TM
    vec = pl.BlockSpec((1, 1, d), lambda i, j: (i // tiles_per_seq, 0, 0))
    return pl.pallas_call(
        _inproj_kernel,
        out_shape=jax.ShapeDtypeStruct((ntok, D_IN_PAD), BF16),
        grid=(ntok // IN_TM, D_IN_PAD // IN_TN),
        in_specs=[
            pl.BlockSpec((IN_TM, d), lambda i, j: (i, 0)),
            pl.BlockSpec((1, d), lambda i, j: (0, 0)),
            vec, vec,
            pl.BlockSpec((1, d, IN_TN), lambda i, j: (layer, 0, j)),
        ],
        out_specs=pl.BlockSpec((IN_TM, IN_TN), lambda i, j: (i, j)),
        scratch_shapes=[pltpu.VMEM((IN_TM, d), BF16)],
        compiler_params=_params("arbitrary", "arbitrary"),
        name="in_projection",
    )(x2, norm_g.reshape(1, d), shift, scale, w_in_bf16)


def _pool_kernel(u_ref, g_ref, w_ref, sc_ref, o_ref, ext_ref):
    t = pl.program_id(1)
    tt = u_ref.shape[0]

    @pl.when(t == 0)
    def _():
        ext_ref[0:POOL_HALO, :] = jnp.zeros((POOL_HALO, BRANCH_W), F32)

    @pl.when(t > 0)
    def _():
        ext_ref[0:POOL_HALO, :] = ext_ref[tt:tt + POOL_HALO, :]

    u = u_ref[...].astype(F32)
    ext_ref[POOL_HALO:POOL_HALO + tt, :] = u
    pos = lax.broadcasted_iota(jnp.int32, (tt, POOL_GROUP_W), 0) + t * tt
    outs = []
    for gi, win in enumerate(POOL_WINDOWS):
        c0 = gi * POOL_GROUP_W
        acc = u[:, c0:c0 + POOL_GROUP_W]
        for k in range(1, win):
            acc = acc + ext_ref[POOL_HALO - k:POOL_HALO - k + tt, c0:c0 + POOL_GROUP_W]
        count = jnp.minimum(pos + 1, win).astype(F32)
        d = acc / count - u[:, c0:c0 + POOL_GROUP_W]
        outs.append(_dot(d, w_ref[gi]))
    y = jnp.concatenate(outs, axis=1) * sc_ref[...]
    o_ref[...] = (y * _silu(g_ref[...].astype(F32))).astype(BF16)


def _pool_mixer(proj, pool_w_bf16, pool_scale, batch, seq):
    ntok = proj.shape[0]
    tiles = seq // POOL_TT
    blk = lambda col: pl.BlockSpec((POOL_TT, BRANCH_W), lambda b, t: (b * tiles + t, col))
    return pl.pallas_call(
        _pool_kernel,
        out_shape=jax.ShapeDtypeStruct((ntok, BRANCH_W), BF16),
        grid=(batch, tiles),
        in_specs=[
            blk(COL_POOL_U), blk(COL_POOL_G),
            pl.BlockSpec(pool_w_bf16.shape, lambda b, t: (0, 0, 0)),
            pl.BlockSpec((1, BRANCH_W), lambda b, t: (0, 0)),
        ],
        out_specs=blk(0),
        scratch_shapes=[pltpu.VMEM((POOL_TT + POOL_HALO, BRANCH_W), F32)],
        compiler_params=_params("arbitrary", "arbitrary"),
        name="pool_mixer",
    )(proj, proj, pool_w_bf16, pool_scale.reshape(1, BRANCH_W))


def _ret_log_gamma():
    return [math.log1p(-(2.0 ** (-5.0 - h))) for h in range(RET_HEADS)]


def _ret_kernel(q_ref, k_ref, v_ref, g_ref, cos_ref, sin_ref, mask_ref, qd_ref, kt_ref, cd_ref,
                ng_ref, o_ref, r_ref):
    @pl.when(pl.program_id(1) == 0)
    def _():
        r_ref[...] = jnp.zeros(r_ref.shape, F32)

    dh = RET_HEAD_DIM
    c = RET_CHUNK
    nh = RET_HEADS
    blocks = [(g, h) for g in range(RET_GROUP) for h in range(nh)]
    cosf = cos_ref[...]
    sinf = sin_ref[...]

    def heads(ref, rotate):
        x = ref[...].astype(F32)
        parts = []
        for g, h in blocks:
            xs = x[g * c:(g + 1) * c, h * dh:(h + 1) * dh]
            if rotate:
                rows = slice(g * c, (g + 1) * c)
                xs = xs * cosf[rows] + pltpu.roll(xs, dh // 2, 1) * sinf[rows]
            parts.append(xs)
        return jnp.stack(parts, axis=0)

    qr = heads(q_ref, True)
    kr = heads(k_ref, True) * (dh ** -0.5)
    v = heads(v_ref, False)
    o = _bmm(_bmm_nt(qr, kr) * mask_ref[...], v)
    kv = _bmm(jnp.swapaxes(kr * kt_ref[...], 1, 2), v)
    state = r_ref[...]
    before = []
    for g in range(RET_GROUP):
        before.append(state)
        state = cd_ref[...] * state + kv[g * nh:(g + 1) * nh]
    r_ref[...] = state
    o = o + _bmm(qr * qd_ref[...], jnp.concatenate(before, axis=0))
    xc = o - jnp.mean(o, axis=-1, keepdims=True)
    var = jnp.mean(xc * xc, axis=-1, keepdims=True)
    on = xc * lax.rsqrt(var + RET_NORM_EPS)
    rows = [jnp.concatenate([on[g * nh + h] for h in range(nh)], axis=1) for g in range(RET_GROUP)]
    o = jnp.concatenate(rows, axis=0) * ng_ref[...]
    o_ref[...] = (o * _silu(g_ref[...].astype(F32))).astype(BF16)


def _retention(proj, cosf, sinf, ret_norm_g, batch, seq):
    ntok = proj.shape[0]
    c = RET_CHUNK
    rows = RET_GROUP * c
    steps = seq // rows
    lg = jnp.asarray(_ret_log_gamma(), F32)
    i = jnp.arange(c, dtype=F32)
    diff = i[:, None] - i[None, :]
    mask = jnp.where(diff >= 0, jnp.exp(lg[:, None, None] * jnp.maximum(diff, 0.0)), 0.0)
    ones = jnp.ones((1, 1, RET_HEAD_DIM), F32)
    q_dec = jnp.exp(lg[:, None] * (i + 1.0))[:, :, None] * ones
    k_tail = jnp.exp(lg[:, None] * (c - 1 - i))[:, :, None] * ones
    chunk_decay = jnp.exp(lg * c)[:, None, None] * ones
    per_group = lambda a: jnp.tile(a, (RET_GROUP, 1, 1))
    mask, q_dec, k_tail = per_group(mask), per_group(q_dec), per_group(k_tail)
    blk = lambda col: pl.BlockSpec((rows, BRANCH_W), lambda b, n: (b * steps + n, col))
    tab = pl.BlockSpec((rows, RET_HEAD_DIM), lambda b, n: (b * steps + n, 0))
    const3 = lambda a: pl.BlockSpec(a.shape, lambda b, n: (0, 0, 0))
    return pl.pallas_call(
        _ret_kernel,
        out_shape=jax.ShapeDtypeStruct((ntok, BRANCH_W), BF16),
        grid=(batch, steps),
        in_specs=[
            blk(COL_RET_Q), blk(COL_RET_K), blk(COL_RET_V), blk(COL_RET_G),
            tab, tab, const3(mask), const3(q_dec), const3(k_tail), const3(chunk_decay),
            pl.BlockSpec((1, BRANCH_W), lambda b, n: (0, 0)),
        ],
        out_specs=blk(0),
        scratch_shapes=[pltpu.VMEM((RET_HEADS, RET_HEAD_DIM, RET_HEAD_DIM), F32)],
        compiler_params=_params("arbitrary", "arbitrary"),
        name="retention",
    )(proj, proj, proj, proj, cosf, sinf, mask, q_dec, k_tail, chunk_decay,
      ret_norm_g.reshape(1, BRANCH_W))


def _rwkv_kernel(pr_ref, pk_ref, pv_ref, pg_ref, plo_ref,
                 mur_ref, muk_ref, muv_ref, mulo_ref,
                 w0_ref, w2_ref, a0_ref, a2_ref, kk_ref, ka_ref, rk_ref, lnw_ref, lnb_ref,
                 seg_ref, o_ref,
                 z_ref, prev_r, prev_k, prev_v, prev_lo):
    L = RWKV_CHUNK
    G = RWKV_GROUP
    GL = G * L
    hd = RWKV_HEAD_DIM
    pw = RWKV_PAIR_W
    P = RWKV_PAIRS

    @pl.when(pl.program_id(1) == 0)
    def _():
        z_ref[...] = jnp.zeros(z_ref.shape, F32)
        prev_r[...] = jnp.zeros(prev_r.shape, F32)
        prev_k[...] = jnp.zeros(prev_k.shape, F32)
        prev_v[...] = jnp.zeros(prev_v.shape, F32)
        prev_lo[...] = jnp.zeros(prev_lo.shape, F32)

    def shifted(p_ref, prev_ref, mu_ref):
        p = p_ref[...].astype(F32)
        first = lax.broadcasted_iota(jnp.int32, p.shape, 0) == 0
        p_prev = jnp.where(first, prev_ref[...], pltpu.roll(p, 1, 0))
        prev_ref[...] = p[GL - 1:GL, :]
        return p + (p_prev - p) * mu_ref[...]

    r = shifted(pr_ref, prev_r, mur_ref)
    k = shifted(pk_ref, prev_k, muk_ref)
    v = shifted(pv_ref, prev_v, muv_ref)
    lo = shifted(plo_ref, prev_lo, mulo_ref)

    seg = seg_ref[...]
    slabs = BRANCH_W // pw

    def seg_sum(x, pieces=2):
        n = x.shape[0]
        xs = jnp.concatenate([x[:, s * pw:(s + 1) * pw] for s in range(slabs)], axis=0)
        hi = xs.astype(BF16)
        s2 = jnp.dot(hi, seg, preferred_element_type=F32)
        if pieces == 2:
            lo_part = (xs - hi.astype(F32)).astype(BF16)
            s2 = jnp.dot(lo_part, seg, preferred_element_type=F32) + s2
        return jnp.concatenate([s2[s * n:(s + 1) * n] for s in range(slabs)], axis=1)

    zw = w0_ref[...] + _dot_hilo(jnp.tanh(lo), w2_ref[...])
    neg = -zw
    softplus = jnp.maximum(neg, 0.0) + jnp.log(1.0 + jnp.exp(-jnp.abs(neg)))
    logw = -jnp.exp(-softplus - 0.5)
    lr = jax.nn.sigmoid(a0_ref[...] + _dot_hilo(lo, a2_ref[...]))
    kk = k * kk_ref[...]
    kk = kk * jnp.minimum(lax.rsqrt(seg_sum(kk * kk)), 1e12)
    k = k * (1.0 + (lr - 1.0) * ka_ref[...])
    b_n = kk * lr

    ti = lax.broadcasted_iota(jnp.int32, (G, L, L), 1)
    tj = lax.broadcasted_iota(jnp.int32, (G, L, L), 2)
    tri = (ti >= tj).astype(BF16)
    hi, mid, low = _split3(jnp.stack([logw[g * L:(g + 1) * L] for g in range(G)], axis=0))
    tri_dot = lambda piece: lax.dot_general(tri, piece, (((2,), (1,)), ((0,), (0,))),
                                            preferred_element_type=F32)
    c3 = tri_dot(low) + tri_dot(mid) + tri_dot(hi)
    c = jnp.concatenate([c3[g] for g in range(G)], axis=0)
    c_last = [c[(g + 1) * L - 1:(g + 1) * L, :] for g in range(G)]
    c_end = jnp.concatenate([jnp.broadcast_to(cl, (L, BRANCH_W)) for cl in c_last], axis=0)
    e_c = jnp.exp(c)
    e_nc = jnp.exp(-c)
    e_tail = jnp.exp(c_end - c)
    a_t = -kk * jnp.exp(c - logw)
    b_t = b_n * e_nc
    k_t = k * e_nc
    r_t = r * e_c
    b_h = b_n * e_tail
    k_h = k * e_tail

    lane = lax.broadcasted_iota(jnp.int32, (pw, pw), 1)
    row = lax.broadcasted_iota(jnp.int32, (pw, pw), 0)
    head_mask = ((lane < hd) == (row < L)).astype(F32)
    same_head = (row // L) == (lane // L)
    strict = (same_head & ((row % L) > (lane % L)))[None]
    incl = (same_head & ((row % L) >= (lane % L)))[None]
    same = lambda n: ((row // n) == (lane // n))[None]
    eye = (row == lane)[None]
    eye_f = eye.astype(F32)

    def stack(x, masked=True):
        parts = []
        for g in range(G):
            for p in range(P):
                xs = x[g * L:(g + 1) * L, p * pw:(p + 1) * pw]
                xs = jnp.concatenate([xs, xs], axis=0)
                parts.append(xs * head_mask if masked else xs)
        return jnp.stack(parts, axis=0)

    a_s, r_s = stack(a_t), stack(r_t)
    b_s, k_s = stack(b_t, masked=False), stack(k_t, masked=False)
    bh_s, kh_s, v_s = stack(b_h), stack(k_h), stack(v)
    w_end = jnp.stack([jnp.exp(c_last[g][:, p * pw:(p + 1) * pw])
                       for g in range(G) for p in range(P)], axis=0)

    mm = _bmm_nt(jnp.concatenate([a_s, r_s], axis=1), jnp.concatenate([b_s, k_s], axis=1))
    m_ab = jnp.where(strict, mm[:, 0:pw, 0:pw], 0.0)
    m_ak = jnp.where(strict, mm[:, 0:pw, pw:2 * pw], 0.0)
    a_rb = jnp.where(incl, mm[:, pw:2 * pw, 0:pw], 0.0)
    a_rk = jnp.where(incl, mm[:, pw:2 * pw, pw:2 * pw], 0.0)
    md = jnp.where(same(8), m_ab, 0.0)
    m2 = _bmm(md, md)
    m34 = _bmm(m2, jnp.concatenate([md, m2], axis=2))
    low_powers = eye_f + md + m2 + m34[:, :, 0:pw]
    t_inv = low_powers + _bmm(m34[:, :, pw:2 * pw], low_powers)
    n = 8
    while n < L:
        off = jnp.where(same(2 * n) & jnp.logical_not(same(n)), m_ab, 0.0)
        t_inv = t_inv + _bmm(t_inv, _bmm(off, t_inv))
        n *= 2
    x = _bmm(t_inv, jnp.concatenate([a_s, _bmm(m_ak, v_s)], axis=2))
    rhs = jnp.concatenate([x, jnp.concatenate([jnp.zeros_like(v_s), v_s], axis=2)], axis=1)
    lhs = jnp.concatenate(
        [jnp.swapaxes(jnp.concatenate([bh_s, kh_s], axis=1), 1, 2),
         jnp.concatenate([a_rb, a_rk], axis=2)], axis=1)
    out = _bmm(lhs, rhs)
    phi = out[:, 0:pw, 0:pw] + jnp.where(eye, w_end, 0.0)
    gg = out[:, 0:pw, pw:2 * pw]
    q = r_s + out[:, pw:2 * pw, 0:pw]
    y0 = out[:, pw:2 * pw, pw:2 * pw]

    z = z_ref[...]
    rows = []
    for g in range(G):
        cs = slice(g * P, (g + 1) * P)
        y_s = _bmm(q[cs], z) + y0[cs]
        z = _bmm(phi[cs], z) + gg[cs]
        rows.append(jnp.concatenate([y_s[p, 0:L] + y_s[p, L:2 * L] for p in range(P)], axis=1))
    z_ref[...] = z

    y = jnp.concatenate(rows, axis=0)
    xc = y - seg_sum(y) * (1.0 / hd)
    var = seg_sum(xc * xc, pieces=1) * (1.0 / hd)
    y = xc * lax.rsqrt(var + RWKV_NORM_EPS) * lnw_ref[...] + lnb_ref[...]
    y = y + seg_sum(r * k * rk_ref[...], pieces=1) * v
    o_ref[...] = (y * _silu(pg_ref[...].astype(F32))).astype(BF16)


def _rwkv_mixer(proj, prm, batch, seq):
    ntok = proj.shape[0]
    rows_per_step = RWKV_GROUP * RWKV_CHUNK
    steps = seq // rows_per_step
    w = BRANCH_W
    blk = lambda col: pl.BlockSpec((rows_per_step, w), lambda b, n: (b * steps + n, col))
    vec = lambda width: pl.BlockSpec((1, width), lambda b, n: (0, 0))
    mat = lambda a: pl.BlockSpec(a.shape, lambda b, n: (0, 0))
    head = jnp.arange(RWKV_PAIR_W) // RWKV_HEAD_DIM
    seg = (head[:, None] == head[None, :]).astype(BF16)
    row = lambda a: a.reshape(1, -1)
    return pl.pallas_call(
        _rwkv_kernel,
        out_shape=jax.ShapeDtypeStruct((ntok, w), BF16),
        grid=(batch, steps),
        in_specs=[
            blk(COL_RW_R), blk(COL_RW_K), blk(COL_RW_V), blk(COL_RW_G),
            pl.BlockSpec((rows_per_step, 2 * RWKV_LORA),
                         lambda b, n: (b * steps + n, COL_LORA_128)),
            vec(w), vec(w), vec(w), vec(2 * RWKV_LORA),
            vec(w), mat(prm["w2"]), vec(w), mat(prm["a2"]),
            vec(w), vec(w), vec(w), vec(w), vec(w),
            mat(seg),
        ],
        out_specs=blk(0),
        scratch_shapes=[
            pltpu.VMEM((RWKV_PAIRS, RWKV_PAIR_W, RWKV_PAIR_W), F32),
            pltpu.VMEM((1, w), F32), pltpu.VMEM((1, w), F32), pltpu.VMEM((1, w), F32),
            pltpu.VMEM((1, 2 * RWKV_LORA), F32),
        ],
        compiler_params=_params("arbitrary", "arbitrary"),
        name="rwkv7_mixer",
    )(proj, proj, proj, proj, proj,
      row(prm["mu_r"]), row(prm["mu_k"]), row(prm["mu_v"]), row(prm["mu_lo"]),
      row(prm["w0"]), prm["w2"], row(prm["a0"]), prm["a2"],
      row(prm["k_k"]), row(prm["k_a"]), row(prm["r_k"]), row(prm["ln_w"]), row(prm["ln_b"]),
      seg)


def _merge_kernel(op_ref, or_ref, ow_ref, g0_ref, g1_ref, g2_ref, wb_ref, wo_ref, x_ref,
                  gate_ref, fg_ref, o_ref, wb_bf16, wo_bf16, *, final_norm):
    @pl.when(pl.program_id(0) == 0)
    def _():
        wb_bf16[...] = wb_ref[0].astype(BF16)
        wo_bf16[...] = wo_ref[0].astype(BF16)

    merged = None
    for n, (br_ref, gl_ref) in enumerate(((op_ref, g0_ref), (or_ref, g1_ref), (ow_ref, g2_ref))):
        term = jax.nn.sigmoid(gl_ref[...].astype(F32)) * _dot(br_ref[...], wb_bf16[n])
        merged = term if merged is None else merged + term
    out = x_ref[...] + gate_ref[0] * _dot(merged, wo_bf16[...])
    if final_norm:
        out = out * lax.rsqrt(jnp.mean(out * out, axis=-1, keepdims=True) + NORM_EPS) * fg_ref[...]
    o_ref[...] = out


def _merge(o_pool, o_ret, o_rwkv, proj, w_branch, w_out, layer, x2, gate, final_g, seq,
           final_norm):
    ntok, d = x2.shape
    tm = MERGE_TM
    tiles_per_seq = seq // tm
    br = pl.BlockSpec((tm, BRANCH_W), lambda i: (i, 0))
    gl = lambda n: pl.BlockSpec((tm, d), lambda i: (i, COL_GATES_1024 + n))
    xs = pl.BlockSpec((tm, d), lambda i: (i, 0))
    return pl.pallas_call(
        functools.partial(_merge_kernel, final_norm=final_norm),
        out_shape=jax.ShapeDtypeStruct((ntok, d), F32),
        grid=(ntok // tm,),
        in_specs=[
            br, br, br, gl(0), gl(1), gl(2),
            pl.BlockSpec((1,) + w_branch.shape[1:], lambda i: (layer, 0, 0, 0)),
            pl.BlockSpec((1,) + w_out.shape[1:], lambda i: (layer, 0, 0)),
            xs,
            pl.BlockSpec((1, 1, d), lambda i: (i // tiles_per_seq, 0, 0)),
            pl.BlockSpec((1, d), lambda i: (0, 0)),
        ],
        out_specs=xs,
        scratch_shapes=[pltpu.VMEM(w_branch.shape[1:], BF16), pltpu.VMEM(w_out.shape[1:], BF16)],
        compiler_params=_params("arbitrary"),
        name="merge_out",
    )(o_pool, o_ret, o_rwkv, proj, proj, proj, w_branch, w_out, x2, gate,
      final_g.reshape(1, d))


def _wprep_kernel(w_ref, o_ref):
    w = BRANCH_W
    lo = RWKV_LORA
    r_end = 7 * w
    rest_src = r_end + 2 * w + 2 * lo
    rest_dst = r_end + 2 * w
    rows = w_ref.shape[1]
    o_ref[0, :, 0:r_end] = w_ref[0, :, 0:r_end].astype(BF16)
    mid = w_ref[0, :, r_end:rest_src]
    o_ref[0, :, r_end:r_end + w] = mid[:, lo:lo + w].astype(BF16)
    o_ref[0, :, r_end + w:rest_dst] = mid[:, lo + w:lo + 2 * w].astype(BF16)
    o_ref[0, :, rest_dst:rest_dst + (D_IN - rest_src)] = w_ref[0, :, rest_src:D_IN].astype(BF16)
    tail = jnp.concatenate([mid[:, 0:lo], mid[:, lo + 2 * w:2 * lo + 2 * w],
                            jnp.zeros((rows, D_IN_PAD - D_IN), F32)], axis=1)
    o_ref[0, :, D_IN - 2 * lo:D_IN_PAD] = tail.astype(BF16)


def _prepare_w_in(w_in):
    depth, d, d_in = w_in.shape
    return pl.pallas_call(
        _wprep_kernel,
        out_shape=jax.ShapeDtypeStruct((depth, d, D_IN_PAD), BF16),
        grid=(depth, d // W_PREP_ROWS),
        in_specs=[pl.BlockSpec((1, W_PREP_ROWS, d_in), lambda l, i: (l, i, 0))],
        out_specs=pl.BlockSpec((1, W_PREP_ROWS, D_IN_PAD), lambda l, i: (l, i, 0)),
        compiler_params=_params("arbitrary", "arbitrary"),
        name="w_in_prep",
    )(w_in)


def kernel(x, c, positions, norm_g, w_ada, b_ada, w_in, pool_w, pool_scale, ret_norm_g,
           rwkv_shift_mu, rwkv_w0, rwkv_w2, rwkv_a0, rwkv_a2, rwkv_k_k, rwkv_k_a, rwkv_r_k,
           rwkv_ln_w, rwkv_ln_b, w_branch, w_out, final_g):
    batch, seq, d = x.shape
    depth = w_in.shape[0]
    ntok = batch * seq
    w = BRANCH_W
    assert d == D_MODEL and w_in.shape[-1] == D_IN
    assert seq % IN_TM == 0 and seq % POOL_TT == 0 and seq % (RET_GROUP * RET_CHUNK) == 0
    assert seq % (RWKV_GROUP * RWKV_CHUNK) == 0 and seq % MERGE_TM == 0

    mod_rows = 8
    c_pad = jnp.zeros((mod_rows, d), F32).at[:batch].set(c)
    mod = _modulation(c_pad, w_ada, b_ada)[:, :batch]
    cosf, sinf = _rope_tables(positions.reshape(ntok, 1))

    w_in_p = _prepare_w_in(w_in)
    pool_w_b = pool_w.astype(BF16)
    lora_pad = jnp.zeros((depth, RWKV_LORA, w), F32)
    w2_pad = jnp.concatenate([rwkv_w2, lora_pad], axis=1)
    a2_pad = jnp.concatenate([lora_pad, rwkv_a2], axis=1)
    mu = rwkv_shift_mu
    mu_lo = jnp.concatenate([mu[:, w:w + RWKV_LORA], mu[:, 3 * w + RWKV_LORA:]], axis=1)

    x2 = x.reshape(ntok, d)
    for l in range(depth):
        shift = mod[l, :, 0:d].reshape(batch, 1, d)
        scale = mod[l, :, d:2 * d].reshape(batch, 1, d)
        gate = mod[l, :, 2 * d:].reshape(batch, 1, d)
        proj = _in_projection(x2, norm_g[l], shift, scale, w_in_p, l, seq)
        o_pool = _pool_mixer(proj, pool_w_b[l], pool_scale[l], batch, seq)
        o_ret = _retention(proj, cosf, sinf, ret_norm_g[l], batch, seq)
        prm = dict(mu_r=mu[l, 0:w], mu_k=mu[l, w + RWKV_LORA:2 * w + RWKV_LORA],
                   mu_v=mu[l, 2 * w + RWKV_LORA:3 * w + RWKV_LORA], mu_lo=mu_lo[l],
                   w0=rwkv_w0[l], w2=w2_pad[l], a0=rwkv_a0[l], a2=a2_pad[l],
                   k_k=rwkv_k_k[l], k_a=rwkv_k_a[l], r_k=rwkv_r_k[l],
                   ln_w=rwkv_ln_w[l], ln_b=rwkv_ln_b[l])
        o_rwkv = _rwkv_mixer(proj, prm, batch, seq)
        x2 = _merge(o_pool, o_ret, o_rwkv, proj, w_branch, w_out, l, x2, gate, final_g,
                    seq, final_norm=(l == depth - 1))
    return x2.reshape(batch, seq, d)
```

```python
import functools
import math

import jax
import jax.numpy as jnp
from jax import lax
from jax.experimental import pallas as pl
from jax.experimental.pallas import tpu as pltpu

F32 = jnp.float32
BF16 = jnp.bfloat16

D_MODEL = 1024
BRANCH_W = D_MODEL // 2
N_BRANCHES = 3
POOL_WINDOWS = (2, 4, 8, 16)
POOL_GROUP_W = BRANCH_W // len(POOL_WINDOWS)
POOL_HALO = 16
RET_HEADS = 4
RET_HEAD_DIM = BRANCH_W // RET_HEADS
RET_CHUNK = 128
ROPE_BASE = 10000.0
RWKV_HEAD_DIM = 64
RWKV_HEADS = BRANCH_W // RWKV_HEAD_DIM
RWKV_LORA = 64
RWKV_CHUNK = 64
RWKV_GROUP = 8
RWKV_PAIR_W = 2 * RWKV_HEAD_DIM
RWKV_PAIRS = RWKV_HEADS // 2
NORM_EPS = 1e-6
RET_NORM_EPS = 1e-5
RWKV_NORM_EPS = 64e-5

COL_POOL_U, COL_POOL_G = 0, 1
COL_RET_Q, COL_RET_K, COL_RET_V, COL_RET_G = 2, 3, 4, 5
COL_RW_R, COL_RW_K, COL_RW_V, COL_RW_G = 6, 7, 8, 9
COL_GATES_1024 = 5
COL_LORA_128 = 64
D_IN = 10 * BRANCH_W + N_BRANCHES * D_MODEL + 2 * RWKV_LORA
MXU_WIDTH = 256
D_IN_PAD = -(-D_IN // MXU_WIDTH) * MXU_WIDTH

VMEM_LIMIT_BYTES = 56 * 1024 * 1024

IN_TM, IN_TN = 1024, D_IN_PAD // 3
IN_SUB_TN = 3 * MXU_WIDTH
W_PREP_ROWS = 256
RET_GROUP = 4
MIX_TM = RET_GROUP * RET_CHUNK
ROPE_TT = 1024


def _dot(a, b):
    return jnp.dot(a.astype(BF16), b.astype(BF16), preferred_element_type=F32)


def _bmm(a, b):
    return lax.dot_general(a.astype(BF16), b.astype(BF16), (((2,), (1,)), ((0,), (0,))),
                           preferred_element_type=F32)


def _bmm_nt(a, b):
    return lax.dot_general(a.astype(BF16), b.astype(BF16), (((2,), (2,)), ((0,), (0,))),
                           preferred_element_type=F32)


def _split3(x):
    hi = x.astype(BF16)
    r1 = x - hi.astype(F32)
    mid = r1.astype(BF16)
    lo = (r1 - mid.astype(F32)).astype(BF16)
    return hi, mid, lo


def _dot_hilo(a, b):
    a_hi = a.astype(BF16)
    a_lo = (a - a_hi.astype(F32)).astype(BF16)
    b_hi = b.astype(BF16)
    b_lo = (b - b_hi.astype(F32)).astype(BF16)
    acc = jnp.dot(a_lo, b_hi, preferred_element_type=F32)
    acc = acc + jnp.dot(a_hi, b_lo, preferred_element_type=F32)
    return acc + jnp.dot(a_hi, b_hi, preferred_element_type=F32)


def _silu(x):
    return x * jax.nn.sigmoid(x)


def _params(*sem):
    return pltpu.CompilerParams(dimension_semantics=sem, vmem_limit_bytes=VMEM_LIMIT_BYTES)


def _mod_kernel(c_ref, w_ref, b_ref, o_ref):
    o_ref[0] = _dot(_silu(c_ref[...]), w_ref[0]) + b_ref[0]


def _modulation(c_pad, w_ada, b_ada):
    depth, d, d3 = w_ada.shape
    rows = c_pad.shape[0]
    return pl.pallas_call(
        _mod_kernel,
        out_shape=jax.ShapeDtypeStruct((depth, rows, d3), F32),
        grid=(depth, d3 // d),
        in_specs=[
            pl.BlockSpec((rows, d), lambda l, j: (0, 0)),
            pl.BlockSpec((1, d, d), lambda l, j: (l, 0, j)),
            pl.BlockSpec((1, 1, d), lambda l, j: (l, 0, j)),
        ],
        out_specs=pl.BlockSpec((1, rows, d), lambda l, j: (l, 0, j)),
        compiler_params=_params("arbitrary", "arbitrary"),
        name="adaln_mod",
    )(c_pad, w_ada, b_ada.reshape(depth, 1, d3))


def _rope_kernel(pos_ref, freq_ref, sign_ref, cos_ref, sin_ref):
    ang = pos_ref[...].astype(F32) * freq_ref[...]
    cos_ref[...] = jnp.cos(ang)
    sin_ref[...] = jnp.sin(ang) * sign_ref[...]


def _rope_tables(pos_col):
    ntok = pos_col.shape[0]
    half = RET_HEAD_DIM // 2
    freqs = ROPE_BASE ** (-jnp.arange(half, dtype=F32) / half)
    freq2 = jnp.concatenate([freqs, freqs]).reshape(1, RET_HEAD_DIM)
    sign = jnp.concatenate([-jnp.ones((half,), F32), jnp.ones((half,), F32)]).reshape(1, RET_HEAD_DIM)
    vec = pl.BlockSpec((1, RET_HEAD_DIM), lambda i: (0, 0))
    tab = pl.BlockSpec((ROPE_TT, RET_HEAD_DIM), lambda i: (i, 0))
    return pl.pallas_call(
        _rope_kernel,
        out_shape=(jax.ShapeDtypeStruct((ntok, RET_HEAD_DIM), F32),) * 2,
        grid=(ntok // ROPE_TT,),
        in_specs=[pl.BlockSpec((ROPE_TT, 1), lambda i: (i, 0)), vec, vec],
        out_specs=(tab, tab),
        compiler_params=_params("arbitrary"),
        name="rope_tables",
    )(pos_col, freq2, sign)


def _inproj_kernel(x_ref, g_ref, shift_ref, scale_ref, w_ref, o_ref, h_ref):
    @pl.when(pl.program_id(1) == 0)
    def _():
        x = x_ref[...]
        y = x * lax.rsqrt(jnp.mean(x * x, axis=-1, keepdims=True) + NORM_EPS) * g_ref[...]
        h_ref[...] = (y * (1.0 + scale_ref[0]) + shift_ref[0]).astype(BF16)

    h = h_ref[...]
    for c0 in range(0, IN_TN, IN_SUB_TN):
        c1 = min(c0 + IN_SUB_TN, IN_TN)
        o_ref[:, c0:c1] = jnp.dot(h, w_ref[0, :, c0:c1], preferred_element_type=F32).astype(BF16)


def _in_projection(x2, norm_g, shift, scale, w_in_bf16, layer, seq):
    ntok, d = x2.shape
    tiles_per_seq = seq // IN_TM
    vec = pl.BlockSpec((1, 1, d), lambda i, j: (i // tiles_per_seq, 0, 0))
    return pl.pallas_call(
        _inproj_kernel,
        out_shape=jax.ShapeDtypeStruct((ntok, D_IN_PAD), BF16),
        grid=(ntok // IN_TM, D_IN_PAD // IN_TN),
        in_specs=[
            pl.BlockSpec((IN_TM, d), lambda i, j: (i, 0)),
            pl.BlockSpec((1, d), lambda i, j: (0, 0)),
            vec, vec,
            pl.BlockSpec((1, d, IN_TN), lambda i, j: (layer, 0, j)),
        ],
        out_specs=pl.BlockSpec((IN_TM, IN_TN), lambda i, j: (i, j)),
        scratch_shapes=[pltpu.VMEM((IN_TM, d), BF16)],
        compiler_params=_params("arbitrary", "arbitrary"),
        name="in_projection",
    )(x2, norm_g.reshape(1, d), shift, scale, w_in_bf16)


def _pool_branch(u_ref, g_ref, w_ref, sc_ref, ext_ref, tile_in_seq):
    tt = u_ref.shape[0]
    u = u_ref[...].astype(F32)
    ext_ref[POOL_HALO:POOL_HALO + tt, :] = u
    pos = lax.broadcasted_iota(jnp.int32, (tt, POOL_GROUP_W), 0) + tile_in_seq * tt
    outs = []
    for gi, win in enumerate(POOL_WINDOWS):
        c0 = gi * POOL_GROUP_W
        acc = u[:, c0:c0 + POOL_GROUP_W]
        for k in range(1, win):
            acc = acc + ext_ref[POOL_HALO - k:POOL_HALO - k + tt, c0:c0 + POOL_GROUP_W]
        count = jnp.minimum(pos + 1, win).astype(F32)
        d = acc / count - u[:, c0:c0 + POOL_GROUP_W]
        outs.append(_dot(d, w_ref[0, gi]))
    y = jnp.concatenate(outs, axis=1) * sc_ref[...]
    return y * _silu(g_ref[...].astype(F32))


def _ret_log_gamma():
    return [math.log1p(-(2.0 ** (-5.0 - h))) for h in range(RET_HEADS)]


def _ret_tables():
    c = RET_CHUNK
    lg = jnp.asarray(_ret_log_gamma(), F32)
    i = jnp.arange(c, dtype=F32)
    diff = i[:, None] - i[None, :]
    mask = jnp.where(diff >= 0, jnp.exp(lg[:, None, None] * jnp.maximum(diff, 0.0)), 0.0)
    ones = jnp.ones((1, 1, RET_HEAD_DIM), F32)
    q_dec = jnp.exp(lg[:, None] * (i + 1.0))[:, :, None] * ones
    k_tail = jnp.exp(lg[:, None] * (c - 1 - i))[:, :, None] * ones
    chunk_decay = jnp.exp(lg * c)[:, None, None] * ones
    per_group = lambda a: jnp.tile(a, (RET_GROUP, 1, 1))
    return per_group(mask), per_group(q_dec), per_group(k_tail), chunk_decay


def _ret_branch(q_ref, k_ref, v_ref, g_ref, cos_ref, sin_ref, mask_ref, qd_ref, kt_ref, cd_ref,
                ng_ref, r_ref):
    dh = RET_HEAD_DIM
    c = RET_CHUNK
    nh = RET_HEADS
    blocks = [(g, h) for g in range(RET_GROUP) for h in range(nh)]
    cosf = cos_ref[...]
    sinf = sin_ref[...]

    def heads(ref, rotate):
        x = ref[...].astype(F32)
        parts = []
        for g, h in blocks:
            xs = x[g * c:(g + 1) * c, h * dh:(h + 1) * dh]
            if rotate:
                rows = slice(g * c, (g + 1) * c)
                xs = xs * cosf[rows] + pltpu.roll(xs, dh // 2, 1) * sinf[rows]
            parts.append(xs)
        return jnp.stack(parts, axis=0)

    qr = heads(q_ref, True)
    kr = heads(k_ref, True) * (dh ** -0.5)
    v = heads(v_ref, False)
    o = _bmm(_bmm_nt(qr, kr) * mask_ref[...], v)
    kv = _bmm(jnp.swapaxes(kr * kt_ref[...], 1, 2), v)
    state = r_ref[...]
    before = []
    for g in range(RET_GROUP):
        before.append(state)
        state = cd_ref[...] * state + kv[g * nh:(g + 1) * nh]
    r_ref[...] = state
    o = o + _bmm(qr * qd_ref[...], jnp.concatenate(before, axis=0))
    xc = o - jnp.mean(o, axis=-1, keepdims=True)
    var = jnp.mean(xc * xc, axis=-1, keepdims=True)
    on = xc * lax.rsqrt(var + RET_NORM_EPS)
    rows = [jnp.concatenate([on[g * nh + h] for h in range(nh)], axis=1) for g in range(RET_GROUP)]
    o = jnp.concatenate(rows, axis=0) * ng_ref[...]
    return o * _silu(g_ref[...].astype(F32))


def _rwkv_kernel(pr_ref, pk_ref, pv_ref, pg_ref, plo_ref,
                 mur_ref, muk_ref, muv_ref, mulo_ref,
                 w0_ref, w2_ref, a0_ref, a2_ref, kk_ref, ka_ref, rk_ref, lnw_ref, lnb_ref,
                 seg_ref, o_ref,
                 z_ref, prev_r, prev_k, prev_v, prev_lo):
    L = RWKV_CHUNK
    G = RWKV_GROUP
    GL = G * L
    hd = RWKV_HEAD_DIM
    pw = RWKV_PAIR_W
    P = RWKV_PAIRS

    @pl.when(pl.program_id(1) == 0)
    def _():
        z_ref[...] = jnp.zeros(z_ref.shape, F32)
        prev_r[...] = jnp.zeros(prev_r.shape, F32)
        prev_k[...] = jnp.zeros(prev_k.shape, F32)
        prev_v[...] = jnp.zeros(prev_v.shape, F32)
        prev_lo[...] = jnp.zeros(prev_lo.shape, F32)

    def shifted(p_ref, prev_ref, mu_ref):
        p = p_ref[...].astype(F32)
        first = lax.broadcasted_iota(jnp.int32, p.shape, 0) == 0
        p_prev = jnp.where(first, prev_ref[...], pltpu.roll(p, 1, 0))
        prev_ref[...] = p[GL - 1:GL, :]
        return p + (p_prev - p) * mu_ref[...]

    r = shifted(pr_ref, prev_r, mur_ref)
    k = shifted(pk_ref, prev_k, muk_ref)
    v = shifted(pv_ref, prev_v, muv_ref)
    lo = shifted(plo_ref, prev_lo, mulo_ref)

    seg = seg_ref[...]
    slabs = BRANCH_W // pw

    def seg_sum(x, pieces=2):
        n = x.shape[0]
        xs = jnp.concatenate([x[:, s * pw:(s + 1) * pw] for s in range(slabs)], axis=0)
        hi = xs.astype(BF16)
        s2 = jnp.dot(hi, seg, preferred_element_type=F32)
        if pieces == 2:
            lo_part = (xs - hi.astype(F32)).astype(BF16)
            s2 = jnp.dot(lo_part, seg, preferred_element_type=F32) + s2
        return jnp.concatenate([s2[s * n:(s + 1) * n] for s in range(slabs)], axis=1)

    zw = w0_ref[...] + _dot_hilo(jnp.tanh(lo), w2_ref[...])
    neg = -zw
    softplus = jnp.maximum(neg, 0.0) + jnp.log(1.0 + jnp.exp(-jnp.abs(neg)))
    logw = -jnp.exp(-softplus - 0.5)
    lr = jax.nn.sigmoid(a0_ref[...] + _dot_hilo(lo, a2_ref[...]))
    kk = k * kk_ref[...]
    kk = kk * jnp.minimum(lax.rsqrt(seg_sum(kk * kk)), 1e12)
    k = k * (1.0 + (lr - 1.0) * ka_ref[...])
    b_n = kk * lr

    ti = lax.broadcasted_iota(jnp.int32, (G, L, L), 1)
    tj = lax.broadcasted_iota(jnp.int32, (G, L, L), 2)
    tri = (ti >= tj).astype(BF16)
    hi, mid, low = _split3(jnp.stack([logw[g * L:(g + 1) * L] for g in range(G)], axis=0))
    tri_dot = lambda piece: lax.dot_general(tri, piece, (((2,), (1,)), ((0,), (0,))),
                                            preferred_element_type=F32)
    c3 = tri_dot(low) + tri_dot(mid) + tri_dot(hi)
    c = jnp.concatenate([c3[g] for g in range(G)], axis=0)
    c_last = [c[(g + 1) * L - 1:(g + 1) * L, :] for g in range(G)]
    c_end = jnp.concatenate([jnp.broadcast_to(cl, (L, BRANCH_W)) for cl in c_last], axis=0)
    e_c = jnp.exp(c)
    e_nc = jnp.exp(-c)
    e_tail = jnp.exp(c_end - c)
    a_t = -kk * jnp.exp(c - logw)
    b_t = b_n * e_nc
    k_t = k * e_nc
    r_t = r * e_c
    b_h = b_n * e_tail
    k_h = k * e_tail

    lane = lax.broadcasted_iota(jnp.int32, (pw, pw), 1)
    row = lax.broadcasted_iota(jnp.int32, (pw, pw), 0)
    head_mask = ((lane < hd) == (row < L)).astype(F32)
    same_head = (row // L) == (lane // L)
    strict = (same_head & ((row % L) > (lane % L)))[None]
    incl = (same_head & ((row % L) >= (lane % L)))[None]
    same = lambda n: ((row // n) == (lane // n))[None]
    eye = (row == lane)[None]
    eye_f = eye.astype(F32)

    def stack(x, masked=True):
        parts = []
        for g in range(G):
            for p in range(P):
                xs = x[g * L:(g + 1) * L, p * pw:(p + 1) * pw]
                xs = jnp.concatenate([xs, xs], axis=0)
                parts.append(xs * head_mask if masked else xs)
        return jnp.stack(parts, axis=0)

    a_s, r_s = stack(a_t), stack(r_t)
    b_s, k_s = stack(b_t, masked=False), stack(k_t, masked=False)
    bh_s, kh_s, v_s = stack(b_h), stack(k_h), stack(v)
    w_end = jnp.stack([jnp.exp(c_last[g][:, p * pw:(p + 1) * pw])
                       for g in range(G) for p in range(P)], axis=0)

    mm = _bmm_nt(jnp.concatenate([a_s, r_s], axis=1), jnp.concatenate([b_s, k_s], axis=1))
    m_ab = jnp.where(strict, mm[:, 0:pw, 0:pw], 0.0)
    m_ak = jnp.where(strict, mm[:, 0:pw, pw:2 * pw], 0.0)
    a_rb = jnp.where(incl, mm[:, pw:2 * pw, 0:pw], 0.0)
    a_rk = jnp.where(incl, mm[:, pw:2 * pw, pw:2 * pw], 0.0)
    md = jnp.where(same(8), m_ab, 0.0)
    m2 = _bmm(md, md)
    m34 = _bmm(m2, jnp.concatenate([md, m2], axis=2))
    low_powers = eye_f + md + m2 + m34[:, :, 0:pw]
    t_inv = low_powers + _bmm(m34[:, :, pw:2 * pw], low_powers)
    n = 8
    while n < L:
        off = jnp.where(same(2 * n) & jnp.logical_not(same(n)), m_ab, 0.0)
        t_inv = t_inv + _bmm(t_inv, _bmm(off, t_inv))
        n *= 2
    x = _bmm(t_inv, jnp.concatenate([a_s, _bmm(m_ak, v_s)], axis=2))
    rhs = jnp.concatenate([x, jnp.concatenate([jnp.zeros_like(v_s), v_s], axis=2)], axis=1)
    lhs = jnp.concatenate(
        [jnp.swapaxes(jnp.concatenate([bh_s, kh_s], axis=1), 1, 2),
         jnp.concatenate([a_rb, a_rk], axis=2)], axis=1)
    out = _bmm(lhs, rhs)
    phi = out[:, 0:pw, 0:pw] + jnp.where(eye, w_end, 0.0)
    gg = out[:, 0:pw, pw:2 * pw]
    q = r_s + out[:, pw:2 * pw, 0:pw]
    y0 = out[:, pw:2 * pw, pw:2 * pw]

    z = z_ref[...]
    rows = []
    for g in range(G):
        cs = slice(g * P, (g + 1) * P)
        y_s = _bmm(q[cs], z) + y0[cs]
        z = _bmm(phi[cs], z) + gg[cs]
        rows.append(jnp.concatenate([y_s[p, 0:L] + y_s[p, L:2 * L] for p in range(P)], axis=1))
    z_ref[...] = z

    y = jnp.concatenate(rows, axis=0)
    xc = y - seg_sum(y) * (1.0 / hd)
    var = seg_sum(xc * xc, pieces=1) * (1.0 / hd)
    y = xc * lax.rsqrt(var + RWKV_NORM_EPS) * lnw_ref[...] + lnb_ref[...]
    y = y + seg_sum(r * k * rk_ref[...], pieces=1) * v
    o_ref[...] = (y * _silu(pg_ref[...].astype(F32))).astype(BF16)


def _rwkv_mixer(proj, prm, batch, seq):
    ntok = proj.shape[0]
    rows_per_step = RWKV_GROUP * RWKV_CHUNK
    steps = seq // rows_per_step
    w = BRANCH_W
    blk = lambda col: pl.BlockSpec((rows_per_step, w), lambda b, n: (b * steps + n, col))
    vec = lambda width: pl.BlockSpec((1, width), lambda b, n: (0, 0))
    mat = lambda a: pl.BlockSpec(a.shape, lambda b, n: (0, 0))
    head = jnp.arange(RWKV_PAIR_W) // RWKV_HEAD_DIM
    seg = (head[:, None] == head[None, :]).astype(BF16)
    row = lambda a: a.reshape(1, -1)
    return pl.pallas_call(
        _rwkv_kernel,
        out_shape=jax.ShapeDtypeStruct((ntok, w), BF16),
        grid=(batch, steps),
        in_specs=[
            blk(COL_RW_R), blk(COL_RW_K), blk(COL_RW_V), blk(COL_RW_G),
            pl.BlockSpec((rows_per_step, 2 * RWKV_LORA),
                         lambda b, n: (b * steps + n, COL_LORA_128)),
            vec(w), vec(w), vec(w), vec(2 * RWKV_LORA),
            vec(w), mat(prm["w2"]), vec(w), mat(prm["a2"]),
            vec(w), vec(w), vec(w), vec(w), vec(w),
            mat(seg),
        ],
        out_specs=blk(0),
        scratch_shapes=[
            pltpu.VMEM((RWKV_PAIRS, RWKV_PAIR_W, RWKV_PAIR_W), F32),
            pltpu.VMEM((1, w), F32), pltpu.VMEM((1, w), F32), pltpu.VMEM((1, w), F32),
            pltpu.VMEM((1, 2 * RWKV_LORA), F32),
        ],
        compiler_params=_params("arbitrary", "arbitrary"),
        name="rwkv7_mixer",
    )(proj, proj, proj, proj, proj,
      row(prm["mu_r"]), row(prm["mu_k"]), row(prm["mu_v"]), row(prm["mu_lo"]),
      row(prm["w0"]), prm["w2"], row(prm["a0"]), prm["a2"],
      row(prm["k_k"]), row(prm["k_a"]), row(prm["r_k"]), row(prm["ln_w"]), row(prm["ln_b"]),
      seg)


def _mix_merge_kernel(pu_ref, pg_ref, pw_ref, psc_ref,
                      q_ref, k_ref, v_ref, rg_ref, cos_ref, sin_ref, mask_ref, qd_ref, kt_ref,
                      cd_ref, ng_ref,
                      ow_ref, g0_ref, g1_ref, g2_ref, wb_ref, wo_ref, x_ref, gate_ref, fg_ref,
                      o_ref, ext_ref, r_ref, *, tiles_per_seq, final_norm):
    tile_in_seq = pl.program_id(0) % tiles_per_seq
    tm = x_ref.shape[0]

    @pl.when(tile_in_seq == 0)
    def _():
        ext_ref[0:POOL_HALO, :] = jnp.zeros((POOL_HALO, BRANCH_W), F32)
        r_ref[...] = jnp.zeros(r_ref.shape, F32)

    @pl.when(tile_in_seq > 0)
    def _():
        ext_ref[0:POOL_HALO, :] = ext_ref[tm:tm + POOL_HALO, :]

    gated = lambda gl_ref, branch, n: (jax.nn.sigmoid(gl_ref[...].astype(F32))
                                       * _dot(branch, wb_ref[0, n]))
    merged = gated(g2_ref, ow_ref[...], 2)
    o_ret = _ret_branch(q_ref, k_ref, v_ref, rg_ref, cos_ref, sin_ref, mask_ref, qd_ref, kt_ref,
                        cd_ref, ng_ref, r_ref)
    merged = merged + gated(g1_ref, o_ret, 1)
    o_pool = _pool_branch(pu_ref, pg_ref, pw_ref, psc_ref, ext_ref, tile_in_seq)
    merged = merged + gated(g0_ref, o_pool, 0)
    out = x_ref[...] + gate_ref[0] * _dot(merged, wo_ref[0])
    if final_norm:
        out = out * lax.rsqrt(jnp.mean(out * out, axis=-1, keepdims=True) + NORM_EPS) * fg_ref[...]
    o_ref[...] = out


def _mix_merge(proj, o_rwkv, cosf, sinf, pool_w_bf16, pool_scale, ret_norm_g, w_branch_bf16,
               w_out_bf16, layer, x2, gate, final_g, seq, final_norm):
    ntok, d = x2.shape
    tm = MIX_TM
    tiles_per_seq = seq // tm
    mask, q_dec, k_tail, chunk_decay = _ret_tables()
    col = lambda c: pl.BlockSpec((tm, BRANCH_W), lambda i: (i, c))
    gl = lambda n: pl.BlockSpec((tm, d), lambda i: (i, COL_GATES_1024 + n))
    xs = pl.BlockSpec((tm, d), lambda i: (i, 0))
    tab = pl.BlockSpec((tm, RET_HEAD_DIM), lambda i: (i, 0))
    const3 = lambda a: pl.BlockSpec(a.shape, lambda i: (0, 0, 0))
    vec = lambda width: pl.BlockSpec((1, width), lambda i: (0, 0))
    return pl.pallas_call(
        functools.partial(_mix_merge_kernel, tiles_per_seq=tiles_per_seq, final_norm=final_norm),
        out_shape=jax.ShapeDtypeStruct((ntok, d), F32),
        grid=(ntok // tm,),
        in_specs=[
            col(COL_POOL_U), col(COL_POOL_G),
            pl.BlockSpec((1,) + pool_w_bf16.shape[1:], lambda i: (layer, 0, 0, 0)),
            vec(BRANCH_W),
            col(COL_RET_Q), col(COL_RET_K), col(COL_RET_V), col(COL_RET_G), tab, tab,
            const3(mask), const3(q_dec), const3(k_tail), const3(chunk_decay), vec(BRANCH_W),
            col(0), gl(0), gl(1), gl(2),
            pl.BlockSpec((1,) + w_branch_bf16.shape[1:], lambda i: (layer, 0, 0, 0)),
            pl.BlockSpec((1,) + w_out_bf16.shape[1:], lambda i: (layer, 0, 0)),
            xs,
            pl.BlockSpec((1, 1, d), lambda i: (i // tiles_per_seq, 0, 0)),
            vec(d),
        ],
        out_specs=xs,
        scratch_shapes=[pltpu.VMEM((tm + POOL_HALO, BRANCH_W), F32),
                        pltpu.VMEM((RET_HEADS, RET_HEAD_DIM, RET_HEAD_DIM), F32)],
        compiler_params=_params("arbitrary"),
        name="mix_merge",
    )(proj, proj, pool_w_bf16, pool_scale.reshape(1, BRANCH_W),
      proj, proj, proj, proj, cosf, sinf, mask, q_dec, k_tail, chunk_decay,
      ret_norm_g.reshape(1, BRANCH_W),
      o_rwkv, proj, proj, proj, w_branch_bf16, w_out_bf16, x2, gate, final_g.reshape(1, d))


def _cast_kernel(x_ref, o_ref):
    o_ref[...] = x_ref[...].astype(o_ref.dtype)


def _to_bf16(w):
    lead = w.shape[:-2]
    tail = w.shape[-2:]
    flat = w.reshape((-1,) + tail)
    spec = pl.BlockSpec((1,) + tail, lambda i: (i, 0, 0))
    out = pl.pallas_call(
        _cast_kernel,
        out_shape=jax.ShapeDtypeStruct(flat.shape, BF16),
        grid=(flat.shape[0],),
        in_specs=[spec],
        out_specs=spec,
        compiler_params=_params("arbitrary"),
        name="weights_to_bf16",
    )(flat)
    return out.reshape(lead + tail)


def _wprep_kernel(w_ref, o_ref):
    w = BRANCH_W
    lo = RWKV_LORA
    r_end = 7 * w
    rest_src = r_end + 2 * w + 2 * lo
    rest_dst = r_end + 2 * w
    rows = w_ref.shape[1]
    o_ref[0, :, 0:r_end] = w_ref[0, :, 0:r_end].astype(BF16)
    mid = w_ref[0, :, r_end:rest_src]
    o_ref[0, :, r_end:r_end + w] = mid[:, lo:lo + w].astype(BF16)
    o_ref[0, :, r_end + w:rest_dst] = mid[:, lo + w:lo + 2 * w].astype(BF16)
    o_ref[0, :, rest_dst:rest_dst + (D_IN - rest_src)] = w_ref[0, :, rest_src:D_IN].astype(BF16)
    tail = jnp.concatenate([mid[:, 0:lo], mid[:, lo + 2 * w:2 * lo + 2 * w],
                            jnp.zeros((rows, D_IN_PAD - D_IN), F32)], axis=1)
    o_ref[0, :, D_IN - 2 * lo:D_IN_PAD] = tail.astype(BF16)


def _prepare_w_in(w_in):
    depth, d, d_in = w_in.shape
    return pl.pallas_call(
        _wprep_kernel,
        out_shape=jax.ShapeDtypeStruct((depth, d, D_IN_PAD), BF16),
        grid=(depth, d // W_PREP_ROWS),
        in_specs=[pl.BlockSpec((1, W_PREP_ROWS, d_in), lambda l, i: (l, i, 0))],
        out_specs=pl.BlockSpec((1, W_PREP_ROWS, D_IN_PAD), lambda l, i: (l, i, 0)),
        compiler_params=_params("arbitrary", "arbitrary"),
        name="w_in_prep",
    )(w_in)


def kernel(x, c, positions, norm_g, w_ada, b_ada, w_in, pool_w, pool_scale, ret_norm_g,
           rwkv_shift_mu, rwkv_w0, rwkv_w2, rwkv_a0, rwkv_a2, rwkv_k_k, rwkv_k_a, rwkv_r_k,
           rwkv_ln_w, rwkv_ln_b, w_branch, w_out, final_g):
    batch, seq, d = x.shape
    depth = w_in.shape[0]
    ntok = batch * seq
    w = BRANCH_W
    assert d == D_MODEL and w_in.shape[-1] == D_IN
    assert seq % IN_TM == 0 and seq % MIX_TM == 0 and seq % (RWKV_GROUP * RWKV_CHUNK) == 0

    mod_rows = 8
    c_pad = jnp.zeros((mod_rows, d), F32).at[:batch].set(c)
    mod = _modulation(c_pad, w_ada, b_ada)[:, :batch]
    cosf, sinf = _rope_tables(positions.reshape(ntok, 1))

    w_in_p = _prepare_w_in(w_in)
    pool_w_b = _to_bf16(pool_w)
    w_branch_b = _to_bf16(w_branch)
    w_out_b = _to_bf16(w_out)
    lora_pad = jnp.zeros((depth, RWKV_LORA, w), F32)
    w2_pad = jnp.concatenate([rwkv_w2, lora_pad], axis=1)
    a2_pad = jnp.concatenate([lora_pad, rwkv_a2], axis=1)
    mu = rwkv_shift_mu
    mu_lo = jnp.concatenate([mu[:, w:w + RWKV_LORA], mu[:, 3 * w + RWKV_LORA:]], axis=1)

    x2 = x.reshape(ntok, d)
    for l in range(depth):
        shift = mod[l, :, 0:d].reshape(batch, 1, d)
        scale = mod[l, :, d:2 * d].reshape(batch, 1, d)
        gate = mod[l, :, 2 * d:].reshape(batch, 1, d)
        proj = _in_projection(x2, norm_g[l], shift, scale, w_in_p, l, seq)
        prm = dict(mu_r=mu[l, 0:w], mu_k=mu[l, w + RWKV_LORA:2 * w + RWKV_LORA],
                   mu_v=mu[l, 2 * w + RWKV_LORA:3 * w + RWKV_LORA], mu_lo=mu_lo[l],
                   w0=rwkv_w0[l], w2=w2_pad[l], a0=rwkv_a0[l], a2=a2_pad[l],
                   k_k=rwkv_k_k[l], k_a=rwkv_k_a[l], r_k=rwkv_r_k[l],
                   ln_w=rwkv_ln_w[l], ln_b=rwkv_ln_b[l])
        o_rwkv = _rwkv_mixer(proj, prm, batch, seq)
        x2 = _mix_merge(proj, o_rwkv, cosf, sinf, pool_w_b, pool_scale[l], ret_norm_g[l], w_branch_b,
                        w_out_b, l, x2, gate, final_g, seq, final_norm=(l == depth - 1))
    return x2.reshape(batch, seq, d)
```

```python
import functools
import math

import jax
import jax.numpy as jnp
from jax import lax
from jax.experimental import pallas as pl
from jax.experimental.pallas import tpu as pltpu

F32 = jnp.float32
BF16 = jnp.bfloat16

D_MODEL = 1024
BRANCH_W = D_MODEL // 2
N_BRANCHES = 3
POOL_WINDOWS = (2, 4, 8, 16)
POOL_GROUP_W = BRANCH_W // len(POOL_WINDOWS)
POOL_HALO = 16
RET_HEADS = 4
RET_HEAD_DIM = BRANCH_W // RET_HEADS
RET_CHUNK = 128
ROPE_BASE = 10000.0
RWKV_HEAD_DIM = 64
RWKV_HEADS = BRANCH_W // RWKV_HEAD_DIM
RWKV_LORA = 64
RWKV_CHUNK = 64
RWKV_GROUP = 8
RWKV_PAIR_W = 2 * RWKV_HEAD_DIM
RWKV_PAIRS = RWKV_HEADS // 2
LOG2E = math.log2(math.e)
NORM_EPS = 1e-6
RET_NORM_EPS = 1e-5
RWKV_NORM_EPS = 64e-5

COL_POOL_U, COL_POOL_G = 0, 1
COL_RET_Q, COL_RET_K, COL_RET_V, COL_RET_G = 2, 3, 4, 5
COL_RW_R, COL_RW_K, COL_RW_V, COL_RW_G = 6, 7, 8, 9
COL_GATES_1024 = 5
COL_LORA_128 = 64
D_IN = 10 * BRANCH_W + N_BRANCHES * D_MODEL + 2 * RWKV_LORA
MXU_WIDTH = 256
D_IN_PAD = -(-D_IN // MXU_WIDTH) * MXU_WIDTH

VMEM_LIMIT_BYTES = 56 * 1024 * 1024

IN_TM, IN_TN = 1024, D_IN_PAD // 3
IN_SUB_TN = 3 * MXU_WIDTH
W_PREP_ROWS = 256
CAST_BLOCK_BYTES = 4 * 1024 * 1024
RET_GROUP = 4
MIX_TM = RET_GROUP * RET_CHUNK
ROPE_TT = 1024


def _dot(a, b):
    return jnp.dot(a.astype(BF16), b.astype(BF16), preferred_element_type=F32)


def _bmm(a, b):
    return lax.dot_general(a.astype(BF16), b.astype(BF16), (((2,), (1,)), ((0,), (0,))),
                           preferred_element_type=F32)


def _bmm_nt(a, b):
    return lax.dot_general(a.astype(BF16), b.astype(BF16), (((2,), (2,)), ((0,), (0,))),
                           preferred_element_type=F32)


def _split3(x):
    hi = x.astype(BF16)
    r1 = x - hi.astype(F32)
    mid = r1.astype(BF16)
    lo = (r1 - mid.astype(F32)).astype(BF16)
    return hi, mid, lo


def _dot_hilo(a, b):
    a_hi = a.astype(BF16)
    a_lo = (a - a_hi.astype(F32)).astype(BF16)
    b_hi = b.astype(BF16)
    b_lo = (b - b_hi.astype(F32)).astype(BF16)
    acc = jnp.dot(a_lo, b_hi, preferred_element_type=F32)
    acc = acc + jnp.dot(a_hi, b_lo, preferred_element_type=F32)
    return acc + jnp.dot(a_hi, b_hi, preferred_element_type=F32)


def _silu(x):
    return x * jax.nn.sigmoid(x)


def _params(*sem):
    return pltpu.CompilerParams(dimension_semantics=sem, vmem_limit_bytes=VMEM_LIMIT_BYTES)


def _mod_kernel(c_ref, w_ref, b_ref, o_ref):
    o_ref[0] = _dot(_silu(c_ref[...]), w_ref[0]) + b_ref[0]


def _modulation(c_pad, w_ada, b_ada):
    depth, d, d3 = w_ada.shape
    rows = c_pad.shape[0]
    return pl.pallas_call(
        _mod_kernel,
        out_shape=jax.ShapeDtypeStruct((depth, rows, d3), F32),
        grid=(depth, d3 // d),
        in_specs=[
            pl.BlockSpec((rows, d), lambda l, j: (0, 0)),
            pl.BlockSpec((1, d, d), lambda l, j: (l, 0, j)),
            pl.BlockSpec((1, 1, d), lambda l, j: (l, 0, j)),
        ],
        out_specs=pl.BlockSpec((1, rows, d), lambda l, j: (l, 0, j)),
        compiler_params=_params("arbitrary", "arbitrary"),
        name="adaln_mod",
    )(c_pad, w_ada, b_ada.reshape(depth, 1, d3))


def _rope_kernel(pos_ref, freq_ref, sign_ref, cos_ref, sin_ref):
    ang = pos_ref[...].astype(F32) * freq_ref[...]
    cos_ref[...] = jnp.cos(ang)
    sin_ref[...] = jnp.sin(ang) * sign_ref[...]


def _rope_tables(pos_col):
    ntok = pos_col.shape[0]
    half = RET_HEAD_DIM // 2
    freqs = ROPE_BASE ** (-jnp.arange(half, dtype=F32) / half)
    freq2 = jnp.concatenate([freqs, freqs]).reshape(1, RET_HEAD_DIM)
    sign = jnp.concatenate([-jnp.ones((half,), F32), jnp.ones((half,), F32)]).reshape(1, RET_HEAD_DIM)
    vec = pl.BlockSpec((1, RET_HEAD_DIM), lambda i: (0, 0))
    tab = pl.BlockSpec((ROPE_TT, RET_HEAD_DIM), lambda i: (i, 0))
    return pl.pallas_call(
        _rope_kernel,
        out_shape=(jax.ShapeDtypeStruct((ntok, RET_HEAD_DIM), F32),) * 2,
        grid=(ntok // ROPE_TT,),
        in_specs=[pl.BlockSpec((ROPE_TT, 1), lambda i: (i, 0)), vec, vec],
        out_specs=(tab, tab),
        compiler_params=_params("arbitrary"),
        name="rope_tables",
    )(pos_col, freq2, sign)


def _inproj_kernel(x_ref, g_ref, shift_ref, scale_ref, w_ref, o_ref, h_ref):
    @pl.when(pl.program_id(1) == 0)
    def _():
        x = x_ref[...]
        y = x * lax.rsqrt(jnp.mean(x * x, axis=-1, keepdims=True) + NORM_EPS) * g_ref[...]
        h_ref[...] = (y * (1.0 + scale_ref[0]) + shift_ref[0]).astype(BF16)

    h = h_ref[...]
    for c0 in range(0, IN_TN, IN_SUB_TN):
        c1 = min(c0 + IN_SUB_TN, IN_TN)
        o_ref[:, c0:c1] = jnp.dot(h, w_ref[0, :, c0:c1], preferred_element_type=F32).astype(BF16)


def _in_projection(x2, norm_g, shift, scale, w_in_bf16, layer, seq):
    ntok, d = x2.shape
    tiles_per_seq = seq // IN_TM
    vec = pl.BlockSpec((1, 1, d), lambda i, j: (i // tiles_per_seq, 0, 0))
    return pl.pallas_call(
        _inproj_kernel,
        out_shape=jax.ShapeDtypeStruct((ntok, D_IN_PAD), BF16),
        grid=(ntok // IN_TM, D_IN_PAD // IN_TN),
        in_specs=[
            pl.BlockSpec((IN_TM, d), lambda i, j: (i, 0)),
            pl.BlockSpec((1, d), lambda i, j: (0, 0)),
            vec, vec,
            pl.BlockSpec((1, d, IN_TN), lambda i, j: (layer, 0, j)),
        ],
        out_specs=pl.BlockSpec((IN_TM, IN_TN), lambda i, j: (i, j)),
        scratch_shapes=[pltpu.VMEM((IN_TM, d), BF16)],
        compiler_params=_params("arbitrary", "arbitrary"),
        name="in_projection",
    )(x2, norm_g.reshape(1, d), shift, scale, w_in_bf16)


def _pool_branch(u_ref, g_ref, w_ref, sc_ref, ext_ref, tile_in_seq):
    tt = u_ref.shape[0]
    u = u_ref[...].astype(F32)
    ext_ref[POOL_HALO:POOL_HALO + tt, :] = u
    pos = lax.broadcasted_iota(jnp.int32, (tt, POOL_GROUP_W), 0) + tile_in_seq * tt
    outs = []
    for gi, win in enumerate(POOL_WINDOWS):
        c0 = gi * POOL_GROUP_W
        acc = u[:, c0:c0 + POOL_GROUP_W]
        for k in range(1, win):
            acc = acc + ext_ref[POOL_HALO - k:POOL_HALO - k + tt, c0:c0 + POOL_GROUP_W]
        count = jnp.minimum(pos + 1, win).astype(F32)
        d = acc / count - u[:, c0:c0 + POOL_GROUP_W]
        outs.append(_dot(d, w_ref[0, gi]))
    y = jnp.concatenate(outs, axis=1) * sc_ref[...]
    return y * _silu(g_ref[...].astype(F32))


def _ret_log_gamma():
    return [math.log1p(-(2.0 ** (-5.0 - h))) for h in range(RET_HEADS)]


def _ret_tables():
    c = RET_CHUNK
    lg = jnp.asarray(_ret_log_gamma(), F32)
    i = jnp.arange(c, dtype=F32)
    diff = i[:, None] - i[None, :]
    mask = jnp.where(diff >= 0, jnp.exp(lg[:, None, None] * jnp.maximum(diff, 0.0)), 0.0)
    ones = jnp.ones((1, 1, RET_HEAD_DIM), F32)
    q_dec = jnp.exp(lg[:, None] * (i + 1.0))[:, :, None] * ones
    k_tail = jnp.exp(lg[:, None] * (c - 1 - i))[:, :, None] * ones
    chunk_decay = jnp.exp(lg * c)[:, None, None] * ones
    per_group = lambda a: jnp.tile(a, (RET_GROUP, 1, 1))
    return per_group(mask), per_group(q_dec), per_group(k_tail), chunk_decay


def _ret_branch(q_ref, k_ref, v_ref, g_ref, cos_ref, sin_ref, mask_ref, qd_ref, kt_ref, cd_ref,
                ng_ref, r_ref):
    dh = RET_HEAD_DIM
    c = RET_CHUNK
    nh = RET_HEADS
    blocks = [(g, h) for g in range(RET_GROUP) for h in range(nh)]
    cosf = cos_ref[...]
    sinf = sin_ref[...]

    def heads(ref, rotate):
        x = ref[...].astype(F32)
        parts = []
        for g, h in blocks:
            xs = x[g * c:(g + 1) * c, h * dh:(h + 1) * dh]
            if rotate:
                rows = slice(g * c, (g + 1) * c)
                xs = xs * cosf[rows] + pltpu.roll(xs, dh // 2, 1) * sinf[rows]
            parts.append(xs)
        return jnp.stack(parts, axis=0)

    qr = heads(q_ref, True)
    kr = heads(k_ref, True) * (dh ** -0.5)
    v = heads(v_ref, False)
    o = _bmm(_bmm_nt(qr, kr) * mask_ref[...], v)
    kv = _bmm(jnp.swapaxes(kr * kt_ref[...], 1, 2), v)
    state = r_ref[...]
    before = []
    for g in range(RET_GROUP):
        before.append(state)
        state = cd_ref[...] * state + kv[g * nh:(g + 1) * nh]
    r_ref[...] = state
    o = o + _bmm(qr * qd_ref[...], jnp.concatenate(before, axis=0))
    xc = o - jnp.mean(o, axis=-1, keepdims=True)
    var = jnp.mean(xc * xc, axis=-1, keepdims=True)
    on = xc * lax.rsqrt(var + RET_NORM_EPS)
    rows = [jnp.concatenate([on[g * nh + h] for h in range(nh)], axis=1) for g in range(RET_GROUP)]
    o = jnp.concatenate(rows, axis=0) * ng_ref[...]
    return o * _silu(g_ref[...].astype(F32))


def _rwkv_branch(pr_ref, pk_ref, pv_ref, pg_ref, plo_ref,
                 mur_ref, muk_ref, muv_ref, mulo_ref,
                 w0_ref, w2_ref, a0_ref, a2_ref, kk_ref, ka_ref, rk_ref, lnw_ref, lnb_ref,
                 seg_ref, z_ref, prev_r, prev_k, prev_v, prev_lo):
    L = RWKV_CHUNK
    G = RWKV_GROUP
    GL = G * L
    hd = RWKV_HEAD_DIM
    pw = RWKV_PAIR_W
    P = RWKV_PAIRS

    def shifted(p_ref, prev_ref, mu_ref):
        p = p_ref[...].astype(F32)
        first = lax.broadcasted_iota(jnp.int32, p.shape, 0) == 0
        p_prev = jnp.where(first, prev_ref[...], pltpu.roll(p, 1, 0))
        prev_ref[...] = p[GL - 1:GL, :]
        return p + (p_prev - p) * mu_ref[...]

    r = shifted(pr_ref, prev_r, mur_ref)
    k = shifted(pk_ref, prev_k, muk_ref)
    v = shifted(pv_ref, prev_v, muv_ref)
    lo = shifted(plo_ref, prev_lo, mulo_ref)

    seg = seg_ref[...]
    slabs = BRANCH_W // pw

    def seg_sum(x, pieces=2):
        n = x.shape[0]
        xs = jnp.concatenate([x[:, s * pw:(s + 1) * pw] for s in range(slabs)], axis=0)
        hi = xs.astype(BF16)
        s2 = jnp.dot(hi, seg, preferred_element_type=F32)
        if pieces == 2:
            lo_part = (xs - hi.astype(F32)).astype(BF16)
            s2 = jnp.dot(lo_part, seg, preferred_element_type=F32) + s2
        return jnp.concatenate([s2[s * n:(s + 1) * n] for s in range(slabs)], axis=1)

    zw = w0_ref[...] + _dot_hilo(jnp.tanh(lo), w2_ref[...])
    w = jnp.minimum(zw, 0.0) - jnp.log(1.0 + jnp.exp2(jnp.abs(zw) * (-LOG2E))) - 0.5
    logw = jnp.exp2(w * LOG2E) * (-LOG2E)
    lr = jax.nn.sigmoid(a0_ref[...] + _dot_hilo(lo, a2_ref[...]))
    kk = k * kk_ref[...]
    kk = kk * jnp.minimum(lax.rsqrt(seg_sum(kk * kk, pieces=1)), 1e12)
    k = k * (1.0 + (lr - 1.0) * ka_ref[...])
    b_n = kk * lr

    ti = lax.broadcasted_iota(jnp.int32, (G, L, L), 1)
    tj = lax.broadcasted_iota(jnp.int32, (G, L, L), 2)
    tri = (ti >= tj).astype(BF16)
    hi, mid, low = _split3(jnp.stack([logw[g * L:(g + 1) * L] for g in range(G)], axis=0))
    tri_dot = lambda piece: lax.dot_general(tri, piece, (((2,), (1,)), ((0,), (0,))),
                                            preferred_element_type=F32)
    c3 = tri_dot(low) + tri_dot(mid) + tri_dot(hi)
    c = jnp.concatenate([c3[g] for g in range(G)], axis=0)
    c_last = [c[(g + 1) * L - 1:(g + 1) * L, :] for g in range(G)]
    c_end = jnp.concatenate([jnp.broadcast_to(cl, (L, BRANCH_W)) for cl in c_last], axis=0)
    e_c = jnp.exp2(c)
    e_nc = jnp.exp2(-c)
    e_tail = jnp.exp2(c_end - c)
    a_t = -kk * jnp.exp2(c - logw)
    b_t = b_n * e_nc
    k_t = k * e_nc
    r_t = r * e_c
    b_h = b_n * e_tail
    k_h = k * e_tail

    lane = lax.broadcasted_iota(jnp.int32, (pw, pw), 1)
    row = lax.broadcasted_iota(jnp.int32, (pw, pw), 0)
    head_mask = ((lane < hd) == (row < L)).astype(F32)
    same_head = (row // L) == (lane // L)
    strict = (same_head & ((row % L) > (lane % L)))[None]
    incl = (same_head & ((row % L) >= (lane % L)))[None]
    same = lambda n: ((row // n) == (lane // n))[None]
    eye = (row == lane)[None]
    eye_f = eye.astype(F32)

    def stack(x, masked=True):
        parts = []
        for g in range(G):
            for p in range(P):
                xs = x[g * L:(g + 1) * L, p * pw:(p + 1) * pw]
                xs = jnp.concatenate([xs, xs], axis=0)
                parts.append(xs * head_mask if masked else xs)
        return jnp.stack(parts, axis=0)

    a_s, r_s = stack(a_t), stack(r_t)
    b_s, k_s = stack(b_t, masked=False), stack(k_t, masked=False)
    bh_s, kh_s, v_s = stack(b_h), stack(k_h), stack(v)
    w_end = jnp.stack([jnp.exp2(c_last[g][:, p * pw:(p + 1) * pw])
                       for g in range(G) for p in range(P)], axis=0)

    mm = _bmm_nt(jnp.concatenate([a_s, r_s], axis=1), jnp.concatenate([b_s, k_s], axis=1))
    m_ab = jnp.where(strict, mm[:, 0:pw, 0:pw], 0.0)
    m_ak = jnp.where(strict, mm[:, 0:pw, pw:2 * pw], 0.0)
    a_rb = jnp.where(incl, mm[:, pw:2 * pw, 0:pw], 0.0)
    a_rk = jnp.where(incl, mm[:, pw:2 * pw, pw:2 * pw], 0.0)
    md = jnp.where(same(8), m_ab, 0.0)
    m2 = _bmm(md, md)
    m34 = _bmm(m2, jnp.concatenate([md, m2], axis=2))
    low_powers = eye_f + md + m2 + m34[:, :, 0:pw]
    t_inv = low_powers + _bmm(m34[:, :, pw:2 * pw], low_powers)
    n = 8
    while n < L:
        off = jnp.where(same(2 * n) & jnp.logical_not(same(n)), m_ab, 0.0)
        t_inv = t_inv + _bmm(t_inv, _bmm(off, t_inv))
        n *= 2
    x = _bmm(t_inv, jnp.concatenate([a_s, _bmm(m_ak, v_s)], axis=2))
    rhs = jnp.concatenate([x, jnp.concatenate([jnp.zeros_like(v_s), v_s], axis=2)], axis=1)
    lhs = jnp.concatenate(
        [jnp.swapaxes(jnp.concatenate([bh_s, kh_s], axis=1), 1, 2),
         jnp.concatenate([a_rb, a_rk], axis=2)], axis=1)
    out = _bmm(lhs, rhs)
    phi = out[:, 0:pw, 0:pw] + jnp.where(eye, w_end, 0.0)
    gg = out[:, 0:pw, pw:2 * pw]
    q = r_s + out[:, pw:2 * pw, 0:pw]
    y0 = out[:, pw:2 * pw, pw:2 * pw]

    z = z_ref[...]
    rows = []
    for g in range(G):
        cs = slice(g * P, (g + 1) * P)
        y_s = _bmm(q[cs], z) + y0[cs]
        z = _bmm(phi[cs], z) + gg[cs]
        rows.append(jnp.concatenate([y_s[p, 0:L] + y_s[p, L:2 * L] for p in range(P)], axis=1))
    z_ref[...] = z

    y = jnp.concatenate(rows, axis=0)
    xc = y - seg_sum(y) * (1.0 / hd)
    var = seg_sum(xc * xc, pieces=1) * (1.0 / hd)
    y = xc * lax.rsqrt(var + RWKV_NORM_EPS) * lnw_ref[...] + lnb_ref[...]
    y = y + seg_sum(r * k * rk_ref[...], pieces=1) * v
    return y * _silu(pg_ref[...].astype(F32))


def _mix_merge_kernel(*refs, tiles_per_seq, final_norm):
    (pu_ref, pg_ref, pw_ref, psc_ref,
     q_ref, k_ref, v_ref, rg_ref, cos_ref, sin_ref, mask_ref, qd_ref, kt_ref, cd_ref, ng_ref) = refs[:15]
    rwkv_in = refs[15:34]
    g0_ref, g1_ref, g2_ref, wb_ref, wo_ref, x_ref, gate_ref, fg_ref, o_ref = refs[34:43]
    ext_ref, r_ref = refs[43:45]
    rwkv_state = refs[45:50]
    tile_in_seq = pl.program_id(0) % tiles_per_seq
    tm = x_ref.shape[0]

    @pl.when(tile_in_seq == 0)
    def _():
        ext_ref[0:POOL_HALO, :] = jnp.zeros((POOL_HALO, BRANCH_W), F32)
        r_ref[...] = jnp.zeros(r_ref.shape, F32)
        for ref in rwkv_state:
            ref[...] = jnp.zeros(ref.shape, F32)

    @pl.when(tile_in_seq > 0)
    def _():
        ext_ref[0:POOL_HALO, :] = ext_ref[tm:tm + POOL_HALO, :]

    gated = lambda gl_ref, branch, n: (jax.nn.sigmoid(gl_ref[...].astype(F32))
                                       * _dot(branch, wb_ref[0, n]))
    o_rwkv = _rwkv_branch(*rwkv_in, *rwkv_state)
    merged = gated(g2_ref, o_rwkv, 2)
    o_ret = _ret_branch(q_ref, k_ref, v_ref, rg_ref, cos_ref, sin_ref, mask_ref, qd_ref, kt_ref,
                        cd_ref, ng_ref, r_ref)
    merged = merged + gated(g1_ref, o_ret, 1)
    o_pool = _pool_branch(pu_ref, pg_ref, pw_ref, psc_ref, ext_ref, tile_in_seq)
    merged = merged + gated(g0_ref, o_pool, 0)
    out = x_ref[...] + gate_ref[0] * _dot(merged, wo_ref[0])
    if final_norm:
        out = out * lax.rsqrt(jnp.mean(out * out, axis=-1, keepdims=True) + NORM_EPS) * fg_ref[...]
    o_ref[...] = out


def _mix_merge(proj, prm, cosf, sinf, pool_w_bf16, pool_scale, ret_norm_g, w_branch_bf16,
               w_out_bf16, layer, x2, gate, final_g, seq, final_norm):
    ntok, d = x2.shape
    tm = MIX_TM
    w = BRANCH_W
    tiles_per_seq = seq // tm
    mask, q_dec, k_tail, chunk_decay = _ret_tables()
    head = jnp.arange(RWKV_PAIR_W) // RWKV_HEAD_DIM
    seg = (head[:, None] == head[None, :]).astype(BF16)
    col = lambda c: pl.BlockSpec((tm, w), lambda i: (i, c))
    gl = lambda n: pl.BlockSpec((tm, d), lambda i: (i, COL_GATES_1024 + n))
    xs = pl.BlockSpec((tm, d), lambda i: (i, 0))
    tab = pl.BlockSpec((tm, RET_HEAD_DIM), lambda i: (i, 0))
    const3 = lambda a: pl.BlockSpec(a.shape, lambda i: (0, 0, 0))
    mat = lambda a: pl.BlockSpec(a.shape, lambda i: (0, 0))
    vec = lambda width: pl.BlockSpec((1, width), lambda i: (0, 0))
    row = lambda a: a.reshape(1, -1)
    return pl.pallas_call(
        functools.partial(_mix_merge_kernel, tiles_per_seq=tiles_per_seq, final_norm=final_norm),
        out_shape=jax.ShapeDtypeStruct((ntok, d), F32),
        grid=(ntok // tm,),
        in_specs=[
            col(COL_POOL_U), col(COL_POOL_G),
            pl.BlockSpec((1,) + pool_w_bf16.shape[1:], lambda i: (layer, 0, 0, 0)),
            vec(w),
            col(COL_RET_Q), col(COL_RET_K), col(COL_RET_V), col(COL_RET_G), tab, tab,
            const3(mask), const3(q_dec), const3(k_tail), const3(chunk_decay), vec(w),
            col(COL_RW_R), col(COL_RW_K), col(COL_RW_V), col(COL_RW_G),
            pl.BlockSpec((tm, 2 * RWKV_LORA), lambda i: (i, COL_LORA_128)),
            vec(w), vec(w), vec(w), vec(2 * RWKV_LORA),
            vec(w), mat(prm["w2"]), vec(w), mat(prm["a2"]),
            vec(w), vec(w), vec(w), vec(w), vec(w),
            mat(seg),
            gl(0), gl(1), gl(2),
            pl.BlockSpec((1,) + w_branch_bf16.shape[1:], lambda i: (layer, 0, 0, 0)),
            pl.BlockSpec((1,) + w_out_bf16.shape[1:], lambda i: (layer, 0, 0)),
            xs,
            pl.BlockSpec((1, 1, d), lambda i: (i // tiles_per_seq, 0, 0)),
            vec(d),
        ],
        out_specs=xs,
        scratch_shapes=[
            pltpu.VMEM((tm + POOL_HALO, w), F32),
            pltpu.VMEM((RET_HEADS, RET_HEAD_DIM, RET_HEAD_DIM), F32),
            pltpu.VMEM((RWKV_PAIRS, RWKV_PAIR_W, RWKV_PAIR_W), F32),
            pltpu.VMEM((1, w), F32), pltpu.VMEM((1, w), F32), pltpu.VMEM((1, w), F32),
            pltpu.VMEM((1, 2 * RWKV_LORA), F32),
        ],
        compiler_params=_params("arbitrary"),
        name="mix_merge",
    )(proj, proj, pool_w_bf16, pool_scale.reshape(1, w),
      proj, proj, proj, proj, cosf, sinf, mask, q_dec, k_tail, chunk_decay,
      ret_norm_g.reshape(1, w),
      proj, proj, proj, proj, proj,
      row(prm["mu_r"]), row(prm["mu_k"]), row(prm["mu_v"]), row(prm["mu_lo"]),
      row(prm["w0"]), prm["w2"], row(prm["a0"]), prm["a2"],
      row(prm["k_k"]), row(prm["k_a"]), row(prm["r_k"]), row(prm["ln_w"]), row(prm["ln_b"]),
      seg,
      proj, proj, proj, w_branch_bf16, w_out_bf16, x2, gate, final_g.reshape(1, d))


def _cast_kernel(x_ref, o_ref):
    o_ref[...] = x_ref[...].astype(o_ref.dtype)


def _to_bf16(w):
    lead = w.shape[:-2]
    tail = w.shape[-2:]
    flat = w.reshape((-1,) + tail)
    per_step = math.gcd(flat.shape[0], max(1, CAST_BLOCK_BYTES // (4 * tail[0] * tail[1])))
    spec = pl.BlockSpec((per_step,) + tail, lambda i: (i, 0, 0))
    out = pl.pallas_call(
        _cast_kernel,
        out_shape=jax.ShapeDtypeStruct(flat.shape, BF16),
        grid=(flat.shape[0] // per_step,),
        in_specs=[spec],
        out_specs=spec,
        compiler_params=_params("arbitrary"),
        name="weights_to_bf16",
    )(flat)
    return out.reshape(lead + tail)


def _wprep_kernel(w_ref, o_ref):
    w = BRANCH_W
    lo = RWKV_LORA
    r_end = 7 * w
    rest_src = r_end + 2 * w + 2 * lo
    rest_dst = r_end + 2 * w
    rows = w_ref.shape[1]
    o_ref[0, :, 0:r_end] = w_ref[0, :, 0:r_end].astype(BF16)
    mid = w_ref[0, :, r_end:rest_src]
    o_ref[0, :, r_end:r_end + w] = mid[:, lo:lo + w].astype(BF16)
    o_ref[0, :, r_end + w:rest_dst] = mid[:, lo + w:lo + 2 * w].astype(BF16)
    o_ref[0, :, rest_dst:rest_dst + (D_IN - rest_src)] = w_ref[0, :, rest_src:D_IN].astype(BF16)
    tail = jnp.concatenate([mid[:, 0:lo], mid[:, lo + 2 * w:2 * lo + 2 * w],
                            jnp.zeros((rows, D_IN_PAD - D_IN), F32)], axis=1)
    o_ref[0, :, D_IN - 2 * lo:D_IN_PAD] = tail.astype(BF16)


def _prepare_w_in(w_in):
    depth, d, d_in = w_in.shape
    return pl.pallas_call(
        _wprep_kernel,
        out_shape=jax.ShapeDtypeStruct((depth, d, D_IN_PAD), BF16),
        grid=(depth, d // W_PREP_ROWS),
        in_specs=[pl.BlockSpec((1, W_PREP_ROWS, d_in), lambda l, i: (l, i, 0))],
        out_specs=pl.BlockSpec((1, W_PREP_ROWS, D_IN_PAD), lambda l, i: (l, i, 0)),
        compiler_params=_params("arbitrary", "arbitrary"),
        name="w_in_prep",
    )(w_in)


def kernel(x, c, positions, norm_g, w_ada, b_ada, w_in, pool_w, pool_scale, ret_norm_g,
           rwkv_shift_mu, rwkv_w0, rwkv_w2, rwkv_a0, rwkv_a2, rwkv_k_k, rwkv_k_a, rwkv_r_k,
           rwkv_ln_w, rwkv_ln_b, w_branch, w_out, final_g):
    batch, seq, d = x.shape
    depth = w_in.shape[0]
    ntok = batch * seq
    w = BRANCH_W
    assert d == D_MODEL and w_in.shape[-1] == D_IN
    assert seq % IN_TM == 0 and seq % MIX_TM == 0 and MIX_TM == RWKV_GROUP * RWKV_CHUNK

    mod_rows = 8
    c_pad = jnp.zeros((mod_rows, d), F32).at[:batch].set(c)
    mod = _modulation(c_pad, w_ada, b_ada)[:, :batch]
    cosf, sinf = _rope_tables(positions.reshape(ntok, 1))

    w_in_p = _prepare_w_in(w_in)
    pool_w_b = _to_bf16(pool_w)
    w_branch_b = _to_bf16(w_branch)
    w_out_b = _to_bf16(w_out)
    lora_pad = jnp.zeros((depth, RWKV_LORA, w), F32)
    w2_pad = jnp.concatenate([rwkv_w2, lora_pad], axis=1)
    a2_pad = jnp.concatenate([lora_pad, rwkv_a2], axis=1)
    mu = rwkv_shift_mu
    mu_lo = jnp.concatenate([mu[:, w:w + RWKV_LORA], mu[:, 3 * w + RWKV_LORA:]], axis=1)

    x2 = x.reshape(ntok, d)
    for l in range(depth):
        shift = mod[l, :, 0:d].reshape(batch, 1, d)
        scale = mod[l, :, d:2 * d].reshape(batch, 1, d)
        gate = mod[l, :, 2 * d:].reshape(batch, 1, d)
        proj = _in_projection(x2, norm_g[l], shift, scale, w_in_p, l, seq)
        prm = dict(mu_r=mu[l, 0:w], mu_k=mu[l, w + RWKV_LORA:2 * w + RWKV_LORA],
                   mu_v=mu[l, 2 * w + RWKV_LORA:3 * w + RWKV_LORA], mu_lo=mu_lo[l],
                   w0=rwkv_w0[l], w2=w2_pad[l], a0=rwkv_a0[l], a2=a2_pad[l],
                   k_k=rwkv_k_k[l], k_a=rwkv_k_a[l], r_k=rwkv_r_k[l],
                   ln_w=rwkv_ln_w[l], ln_b=rwkv_ln_b[l])
        x2 = _mix_merge(proj, prm, cosf, sinf, pool_w_b, pool_scale[l], ret_norm_g[l], w_branch_b,
                        w_out_b, l, x2, gate, final_g, seq, final_norm=(l == depth - 1))
    return x2.reshape(batch, seq, d)
```

```python
import functools
import math

import jax
import jax.numpy as jnp
from jax import lax
from jax.experimental import pallas as pl
from jax.experimental.pallas import tpu as pltpu

F32 = jnp.float32
BF16 = jnp.bfloat16

D_MODEL = 1024
BRANCH_W = D_MODEL // 2
N_BRANCHES = 3
POOL_WINDOWS = (2, 4, 8, 16)
POOL_GROUP_W = BRANCH_W // len(POOL_WINDOWS)
POOL_HALO = 16
RET_HEADS = 4
RET_HEAD_DIM = BRANCH_W // RET_HEADS
RET_CHUNK = 128
ROPE_BASE = 10000.0
RWKV_HEAD_DIM = 64
RWKV_HEADS = BRANCH_W // RWKV_HEAD_DIM
RWKV_LORA = 64
RWKV_CHUNK = 64
RWKV_GROUP = 8
RWKV_PAIR_W = 2 * RWKV_HEAD_DIM
RWKV_PAIRS = RWKV_HEADS // 2
LOG2E = math.log2(math.e)
NORM_EPS = 1e-6
RET_NORM_EPS = 1e-5
RWKV_NORM_EPS = 64e-5

COL_POOL_U, COL_POOL_G = 0, 1
COL_RET_Q, COL_RET_K, COL_RET_V, COL_RET_G = 2, 3, 4, 5
COL_RW_R, COL_RW_K, COL_RW_V, COL_RW_G = 6, 7, 8, 9
COL_GATES_1024 = 5
COL_LORA_128 = 64
D_IN = 10 * BRANCH_W + N_BRANCHES * D_MODEL + 2 * RWKV_LORA
MXU_WIDTH = 256
D_IN_PAD = -(-D_IN // MXU_WIDTH) * MXU_WIDTH

VMEM_LIMIT_BYTES = 56 * 1024 * 1024

IN_TM, IN_TN = 1024, D_IN_PAD // 3
IN_SUB_TN = 3 * MXU_WIDTH
W_PREP_ROWS = 256
CAST_BLOCK_BYTES = 4 * 1024 * 1024
RET_GROUP = 4
MIX_TM = RET_GROUP * RET_CHUNK
ROPE_TT = 1024


def _dot(a, b):
    return jnp.dot(a.astype(BF16), b.astype(BF16), preferred_element_type=F32)


def _bmm(a, b):
    return lax.dot_general(a.astype(BF16), b.astype(BF16), (((2,), (1,)), ((0,), (0,))),
                           preferred_element_type=F32)


def _bmm_nt(a, b):
    return lax.dot_general(a.astype(BF16), b.astype(BF16), (((2,), (2,)), ((0,), (0,))),
                           preferred_element_type=F32)


def _split3(x):
    hi = x.astype(BF16)
    r1 = x - hi.astype(F32)
    mid = r1.astype(BF16)
    lo = (r1 - mid.astype(F32)).astype(BF16)
    return hi, mid, lo


def _dot_hilo(a, b):
    a_hi = a.astype(BF16)
    a_lo = (a - a_hi.astype(F32)).astype(BF16)
    b_hi = b.astype(BF16)
    b_lo = (b - b_hi.astype(F32)).astype(BF16)
    acc = jnp.dot(a_lo, b_hi, preferred_element_type=F32)
    acc = acc + jnp.dot(a_hi, b_lo, preferred_element_type=F32)
    return acc + jnp.dot(a_hi, b_hi, preferred_element_type=F32)


def _silu(x):
    return x * jax.nn.sigmoid(x)


def _params(*sem):
    return pltpu.CompilerParams(dimension_semantics=sem, vmem_limit_bytes=VMEM_LIMIT_BYTES)


def _mod_kernel(c_ref, w_ref, b_ref, o_ref):
    o_ref[0] = _dot(_silu(c_ref[...]), w_ref[0]) + b_ref[0]


def _modulation(c_pad, w_ada, b_ada):
    depth, d, d3 = w_ada.shape
    rows = c_pad.shape[0]
    return pl.pallas_call(
        _mod_kernel,
        out_shape=jax.ShapeDtypeStruct((depth, rows, d3), F32),
        grid=(depth, d3 // d),
        in_specs=[
            pl.BlockSpec((rows, d), lambda l, j: (0, 0)),
            pl.BlockSpec((1, d, d), lambda l, j: (l, 0, j)),
            pl.BlockSpec((1, 1, d), lambda l, j: (l, 0, j)),
        ],
        out_specs=pl.BlockSpec((1, rows, d), lambda l, j: (l, 0, j)),
        compiler_params=_params("arbitrary", "arbitrary"),
        name="adaln_mod",
    )(c_pad, w_ada, b_ada.reshape(depth, 1, d3))


def _rope_kernel(pos_ref, freq_ref, sign_ref, cos_ref, sin_ref):
    ang = pos_ref[...].astype(F32) * freq_ref[...]
    cos_ref[...] = jnp.cos(ang)
    sin_ref[...] = jnp.sin(ang) * sign_ref[...]


def _rope_tables(pos_col):
    ntok = pos_col.shape[0]
    half = RET_HEAD_DIM // 2
    freqs = ROPE_BASE ** (-jnp.arange(half, dtype=F32) / half)
    freq2 = jnp.concatenate([freqs, freqs]).reshape(1, RET_HEAD_DIM)
    sign = jnp.concatenate([-jnp.ones((half,), F32), jnp.ones((half,), F32)]).reshape(1, RET_HEAD_DIM)
    vec = pl.BlockSpec((1, RET_HEAD_DIM), lambda i: (0, 0))
    tab = pl.BlockSpec((ROPE_TT, RET_HEAD_DIM), lambda i: (i, 0))
    return pl.pallas_call(
        _rope_kernel,
        out_shape=(jax.ShapeDtypeStruct((ntok, RET_HEAD_DIM), F32),) * 2,
        grid=(ntok // ROPE_TT,),
        in_specs=[pl.BlockSpec((ROPE_TT, 1), lambda i: (i, 0)), vec, vec],
        out_specs=(tab, tab),
        compiler_params=_params("arbitrary"),
        name="rope_tables",
    )(pos_col, freq2, sign)


def _inproj_kernel(x_ref, g_ref, shift_ref, scale_ref, w_ref, o_ref, h_ref):
    @pl.when(pl.program_id(1) == 0)
    def _():
        x = x_ref[...]
        y = x * lax.rsqrt(jnp.mean(x * x, axis=-1, keepdims=True) + NORM_EPS) * g_ref[...]
        h_ref[...] = (y * (1.0 + scale_ref[0]) + shift_ref[0]).astype(BF16)

    h = h_ref[...]
    for c0 in range(0, IN_TN, IN_SUB_TN):
        c1 = min(c0 + IN_SUB_TN, IN_TN)
        o_ref[:, c0:c1] = jnp.dot(h, w_ref[0, :, c0:c1], preferred_element_type=F32).astype(BF16)


def _in_projection(x2, norm_g, shift, scale, w_in_bf16, layer, seq):
    ntok, d = x2.shape
    tiles_per_seq = seq // IN_TM
    vec = pl.BlockSpec((1, 1, d), lambda i, j: (i // tiles_per_seq, 0, 0))
    return pl.pallas_call(
        _inproj_kernel,
        out_shape=jax.ShapeDtypeStruct((ntok, D_IN_PAD), BF16),
        grid=(ntok // IN_TM, D_IN_PAD // IN_TN),
        in_specs=[
            pl.BlockSpec((IN_TM, d), lambda i, j: (i, 0)),
            pl.BlockSpec((1, d), lambda i, j: (0, 0)),
            vec, vec,
            pl.BlockSpec((1, d, IN_TN), lambda i, j: (layer, 0, j)),
        ],
        out_specs=pl.BlockSpec((IN_TM, IN_TN), lambda i, j: (i, j)),
        scratch_shapes=[pltpu.VMEM((IN_TM, d), BF16)],
        compiler_params=_params("arbitrary", "arbitrary"),
        name="in_projection",
    )(x2, norm_g.reshape(1, d), shift, scale, w_in_bf16)


def _pool_branch(u_ref, g_ref, w_ref, sc_ref, ext_ref, tile_in_seq):
    tt = u_ref.shape[0]
    u = u_ref[...].astype(F32)
    ext_ref[POOL_HALO:POOL_HALO + tt, :] = u
    pos = lax.broadcasted_iota(jnp.int32, (tt, POOL_GROUP_W), 0) + tile_in_seq * tt
    outs = []
    for gi, win in enumerate(POOL_WINDOWS):
        c0 = gi * POOL_GROUP_W
        acc = u[:, c0:c0 + POOL_GROUP_W]
        for k in range(1, win):
            acc = acc + ext_ref[POOL_HALO - k:POOL_HALO - k + tt, c0:c0 + POOL_GROUP_W]
        count = jnp.minimum(pos + 1, win).astype(F32)
        d = acc / count - u[:, c0:c0 + POOL_GROUP_W]
        outs.append(_dot(d, w_ref[0, gi]))
    y = jnp.concatenate(outs, axis=1) * sc_ref[...]
    return y * _silu(g_ref[...].astype(F32))


def _ret_log_gamma():
    return [math.log1p(-(2.0 ** (-5.0 - h))) for h in range(RET_HEADS)]


def _ret_tables():
    c = RET_CHUNK
    lg = jnp.asarray(_ret_log_gamma(), F32)
    i = jnp.arange(c, dtype=F32)
    diff = i[:, None] - i[None, :]
    mask = jnp.where(diff >= 0, jnp.exp(lg[:, None, None] * jnp.maximum(diff, 0.0)), 0.0)
    ones = jnp.ones((1, 1, RET_HEAD_DIM), F32)
    q_dec = jnp.exp(lg[:, None] * (i + 1.0))[:, :, None] * ones
    k_tail = jnp.exp(lg[:, None] * (c - 1 - i))[:, :, None] * ones
    chunk_decay = jnp.exp(lg * c)[:, None, None] * ones
    per_group = lambda a: jnp.tile(a, (RET_GROUP, 1, 1))
    return per_group(mask), per_group(q_dec), per_group(k_tail), chunk_decay


def _ret_branch(q_ref, k_ref, v_ref, g_ref, cos_ref, sin_ref, mask_ref, qd_ref, kt_ref, cd_ref,
                ng_ref, r_ref):
    dh = RET_HEAD_DIM
    c = RET_CHUNK
    nh = RET_HEADS
    blocks = [(g, h) for g in range(RET_GROUP) for h in range(nh)]
    cosf = cos_ref[...]
    sinf = sin_ref[...]

    def heads(ref, rotate):
        x = ref[...].astype(F32)
        parts = []
        for g, h in blocks:
            xs = x[g * c:(g + 1) * c, h * dh:(h + 1) * dh]
            if rotate:
                rows = slice(g * c, (g + 1) * c)
                xs = xs * cosf[rows] + pltpu.roll(xs, dh // 2, 1) * sinf[rows]
            parts.append(xs)
        return jnp.stack(parts, axis=0)

    qr = heads(q_ref, True)
    kr = heads(k_ref, True) * (dh ** -0.5)
    v = heads(v_ref, False)
    o = _bmm(_bmm_nt(qr, kr) * mask_ref[...], v)
    kv = _bmm(jnp.swapaxes(kr * kt_ref[...], 1, 2), v)
    state = r_ref[...]
    before = []
    for g in range(RET_GROUP):
        before.append(state)
        state = cd_ref[...] * state + kv[g * nh:(g + 1) * nh]
    r_ref[...] = state
    o = o + _bmm(qr * qd_ref[...], jnp.concatenate(before, axis=0))
    xc = o - jnp.mean(o, axis=-1, keepdims=True)
    var = jnp.mean(xc * xc, axis=-1, keepdims=True)
    on = xc * lax.rsqrt(var + RET_NORM_EPS)
    rows = [jnp.concatenate([on[g * nh + h] for h in range(nh)], axis=1) for g in range(RET_GROUP)]
    o = jnp.concatenate(rows, axis=0) * ng_ref[...]
    return o * _silu(g_ref[...].astype(F32))


def _rwkv_branch(pr_ref, pk_ref, pv_ref, pg_ref, plo_ref,
                 mur_ref, muk_ref, muv_ref, mulo_ref,
                 w0_ref, w2_ref, a0_ref, a2_ref, kk_ref, ka_ref, rk_ref, lnw_ref, lnb_ref,
                 seg_ref, z_ref, prev_r, prev_k, prev_v, prev_lo):
    L = RWKV_CHUNK
    G = RWKV_GROUP
    hd = RWKV_HEAD_DIM
    pw = RWKV_PAIR_W
    P = RWKV_PAIRS

    seg = seg_ref[...]
    slabs = BRANCH_W // pw

    def seg_sum(x, pieces=2):
        n = x.shape[0]
        xs = jnp.concatenate([x[:, s * pw:(s + 1) * pw] for s in range(slabs)], axis=0)
        hi = xs.astype(BF16)
        s2 = jnp.dot(hi, seg, preferred_element_type=F32)
        if pieces == 2:
            lo_part = (xs - hi.astype(F32)).astype(BF16)
            s2 = jnp.dot(lo_part, seg, preferred_element_type=F32) + s2
        return jnp.concatenate([s2[s * n:(s + 1) * n] for s in range(slabs)], axis=1)

    ti = lax.broadcasted_iota(jnp.int32, (L, L), 0)
    tj = lax.broadcasted_iota(jnp.int32, (L, L), 1)
    tri = (ti >= tj).astype(BF16)
    first_row = lax.broadcasted_iota(jnp.int32, (L, 1), 0) == 0
    streams = ((pr_ref, prev_r, mur_ref), (pk_ref, prev_k, muk_ref), (pv_ref, prev_v, muv_ref),
               (plo_ref, prev_lo, mulo_ref))
    last_rows = [prev_ref[...] for _, prev_ref, _ in streams]
    per_chunk = []
    for g in range(G):
        rows = slice(g * L, (g + 1) * L)
        shifted = []
        for i, (p_ref, _, mu_ref) in enumerate(streams):
            p = p_ref[rows, :].astype(F32)
            p_prev = jnp.where(first_row, last_rows[i], pltpu.roll(p, 1, 0))
            last_rows[i] = p[L - 1:L, :]
            shifted.append(p + (p_prev - p) * mu_ref[...])
        r, k, v, lo = shifted
        zw = w0_ref[...] + _dot_hilo(jnp.tanh(lo), w2_ref[...])
        w = jnp.minimum(zw, 0.0) - jnp.log(1.0 + jnp.exp2(jnp.abs(zw) * (-LOG2E))) - 0.5
        logw = jnp.exp2(w * LOG2E) * (-LOG2E)
        lr = jax.nn.sigmoid(a0_ref[...] + _dot_hilo(lo, a2_ref[...]))
        kk = k * kk_ref[...]
        kk = kk * jnp.minimum(lax.rsqrt(seg_sum(kk * kk, pieces=1)), 1e12)
        k = k * (1.0 + (lr - 1.0) * ka_ref[...])
        b_n = kk * lr
        hi, mid, low = _split3(logw)
        tri_dot = lambda piece: jnp.dot(tri, piece, preferred_element_type=F32)
        c = tri_dot(low) + tri_dot(mid) + tri_dot(hi)
        c_last = c[L - 1:L, :]
        e_nc = jnp.exp2(-c)
        e_tail = jnp.exp2(c_last - c)
        per_chunk.append(dict(
            a_t=-kk * jnp.exp2(c - logw), b_t=b_n * e_nc, k_t=k * e_nc, r_t=r * jnp.exp2(c),
            b_h=b_n * e_tail, k_h=k * e_tail, v=v, w_end=jnp.exp2(c_last),
            bonus=seg_sum(r * k * rk_ref[...], pieces=1) * v))
    for (_, prev_ref, _), last in zip(streams, last_rows):
        prev_ref[...] = last

    lane = lax.broadcasted_iota(jnp.int32, (pw, pw), 1)
    row = lax.broadcasted_iota(jnp.int32, (pw, pw), 0)
    head_mask = ((lane < hd) == (row < L)).astype(F32)
    same_head = (row // L) == (lane // L)
    strict = (same_head & ((row % L) > (lane % L)))[None]
    incl = (same_head & ((row % L) >= (lane % L)))[None]
    same = lambda n: ((row // n) == (lane // n))[None]
    eye = (row == lane)[None]
    eye_f = eye.astype(F32)

    def stack(name, masked=True):
        parts = []
        for g in range(G):
            for p in range(P):
                xs = per_chunk[g][name][:, p * pw:(p + 1) * pw]
                xs = jnp.concatenate([xs, xs], axis=0)
                parts.append(xs * head_mask if masked else xs)
        return jnp.stack(parts, axis=0)

    a_s, r_s = stack("a_t"), stack("r_t")
    b_s, k_s = stack("b_t", masked=False), stack("k_t", masked=False)
    bh_s, kh_s, v_s = stack("b_h"), stack("k_h"), stack("v")
    w_end = jnp.stack([per_chunk[g]["w_end"][:, p * pw:(p + 1) * pw]
                       for g in range(G) for p in range(P)], axis=0)

    mm = _bmm_nt(jnp.concatenate([a_s, r_s], axis=1), jnp.concatenate([b_s, k_s], axis=1))
    m_ab = jnp.where(strict, mm[:, 0:pw, 0:pw], 0.0)
    m_ak = jnp.where(strict, mm[:, 0:pw, pw:2 * pw], 0.0)
    a_rb = jnp.where(incl, mm[:, pw:2 * pw, 0:pw], 0.0)
    a_rk = jnp.where(incl, mm[:, pw:2 * pw, pw:2 * pw], 0.0)
    md = jnp.where(same(8), m_ab, 0.0)
    m2 = _bmm(md, md)
    m34 = _bmm(m2, jnp.concatenate([md, m2], axis=2))
    low_powers = eye_f + md + m2 + m34[:, :, 0:pw]
    t_inv = low_powers + _bmm(m34[:, :, pw:2 * pw], low_powers)
    n = 8
    while n < L:
        off = jnp.where(same(2 * n) & jnp.logical_not(same(n)), m_ab, 0.0)
        t_inv = t_inv + _bmm(t_inv, _bmm(off, t_inv))
        n *= 2
    x = _bmm(t_inv, jnp.concatenate([a_s, _bmm(m_ak, v_s)], axis=2))
    rhs = jnp.concatenate([x, jnp.concatenate([jnp.zeros_like(v_s), v_s], axis=2)], axis=1)
    lhs = jnp.concatenate(
        [jnp.swapaxes(jnp.concatenate([bh_s, kh_s], axis=1), 1, 2),
         jnp.concatenate([a_rb, a_rk], axis=2)], axis=1)
    out = _bmm(lhs, rhs)
    phi = out[:, 0:pw, 0:pw] + jnp.where(eye, w_end, 0.0)
    gg = out[:, 0:pw, pw:2 * pw]
    q = r_s + out[:, pw:2 * pw, 0:pw]
    y0 = out[:, pw:2 * pw, pw:2 * pw]

    z = z_ref[...]
    rows = []
    for g in range(G):
        cs = slice(g * P, (g + 1) * P)
        y_s = _bmm(q[cs], z) + y0[cs]
        z = _bmm(phi[cs], z) + gg[cs]
        rows.append(jnp.concatenate([y_s[p, 0:L] + y_s[p, L:2 * L] for p in range(P)], axis=1))
    z_ref[...] = z

    y = jnp.concatenate(rows, axis=0)
    xc = y - seg_sum(y) * (1.0 / hd)
    var = seg_sum(xc * xc, pieces=1) * (1.0 / hd)
    y = xc * lax.rsqrt(var + RWKV_NORM_EPS) * lnw_ref[...] + lnb_ref[...]
    y = y + jnp.concatenate([chunk["bonus"] for chunk in per_chunk], axis=0)
    return y * _silu(pg_ref[...].astype(F32))


def _mix_merge_kernel(*refs, tiles_per_seq, final_norm):
    (pu_ref, pg_ref, pw_ref, psc_ref,
     q_ref, k_ref, v_ref, rg_ref, cos_ref, sin_ref, mask_ref, qd_ref, kt_ref, cd_ref, ng_ref) = refs[:15]
    rwkv_in = refs[15:34]
    g0_ref, g1_ref, g2_ref, wb_ref, wo_ref, x_ref, gate_ref, fg_ref, o_ref = refs[34:43]
    ext_ref, r_ref = refs[43:45]
    rwkv_state = refs[45:50]
    tile_in_seq = pl.program_id(0) % tiles_per_seq
    tm = x_ref.shape[0]

    @pl.when(tile_in_seq == 0)
    def _():
        ext_ref[0:POOL_HALO, :] = jnp.zeros((POOL_HALO, BRANCH_W), F32)
        r_ref[...] = jnp.zeros(r_ref.shape, F32)
        for ref in rwkv_state:
            ref[...] = jnp.zeros(ref.shape, F32)

    @pl.when(tile_in_seq > 0)
    def _():
        ext_ref[0:POOL_HALO, :] = ext_ref[tm:tm + POOL_HALO, :]

    gated = lambda gl_ref, branch, n: (jax.nn.sigmoid(gl_ref[...].astype(F32))
                                       * _dot(branch, wb_ref[0, n]))
    o_rwkv = _rwkv_branch(*rwkv_in, *rwkv_state)
    merged = gated(g2_ref, o_rwkv, 2)
    o_ret = _ret_branch(q_ref, k_ref, v_ref, rg_ref, cos_ref, sin_ref, mask_ref, qd_ref, kt_ref,
                        cd_ref, ng_ref, r_ref)
    merged = merged + gated(g1_ref, o_ret, 1)
    o_pool = _pool_branch(pu_ref, pg_ref, pw_ref, psc_ref, ext_ref, tile_in_seq)
    merged = merged + gated(g0_ref, o_pool, 0)
    out = x_ref[...] + gate_ref[0] * _dot(merged, wo_ref[0])
    if final_norm:
        out = out * lax.rsqrt(jnp.mean(out * out, axis=-1, keepdims=True) + NORM_EPS) * fg_ref[...]
    o_ref[...] = out


def _mix_merge(proj, prm, cosf, sinf, pool_w_bf16, pool_scale, ret_norm_g, w_branch_bf16,
               w_out_bf16, layer, x2, gate, final_g, seq, final_norm):
    ntok, d = x2.shape
    tm = MIX_TM
    w = BRANCH_W
    tiles_per_seq = seq // tm
    mask, q_dec, k_tail, chunk_decay = _ret_tables()
    head = jnp.arange(RWKV_PAIR_W) // RWKV_HEAD_DIM
    seg = (head[:, None] == head[None, :]).astype(BF16)
    col = lambda c: pl.BlockSpec((tm, w), lambda i: (i, c))
    gl = lambda n: pl.BlockSpec((tm, d), lambda i: (i, COL_GATES_1024 + n))
    xs = pl.BlockSpec((tm, d), lambda i: (i, 0))
    tab = pl.BlockSpec((tm, RET_HEAD_DIM), lambda i: (i, 0))
    const3 = lambda a: pl.BlockSpec(a.shape, lambda i: (0, 0, 0))
    mat = lambda a: pl.BlockSpec(a.shape, lambda i: (0, 0))
    vec = lambda width: pl.BlockSpec((1, width), lambda i: (0, 0))
    row = lambda a: a.reshape(1, -1)
    return pl.pallas_call(
        functools.partial(_mix_merge_kernel, tiles_per_seq=tiles_per_seq, final_norm=final_norm),
        out_shape=jax.ShapeDtypeStruct((ntok, d), F32),
        grid=(ntok // tm,),
        in_specs=[
            col(COL_POOL_U), col(COL_POOL_G),
            pl.BlockSpec((1,) + pool_w_bf16.shape[1:], lambda i: (layer, 0, 0, 0)),
            vec(w),
            col(COL_RET_Q), col(COL_RET_K), col(COL_RET_V), col(COL_RET_G), tab, tab,
            const3(mask), const3(q_dec), const3(k_tail), const3(chunk_decay), vec(w),
            col(COL_RW_R), col(COL_RW_K), col(COL_RW_V), col(COL_RW_G),
            pl.BlockSpec((tm, 2 * RWKV_LORA), lambda i: (i, COL_LORA_128)),
            vec(w), vec(w), vec(w), vec(2 * RWKV_LORA),
            vec(w), mat(prm["w2"]), vec(w), mat(prm["a2"]),
            vec(w), vec(w), vec(w), vec(w), vec(w),
            mat(seg),
            gl(0), gl(1), gl(2),
            pl.BlockSpec((1,) + w_branch_bf16.shape[1:], lambda i: (layer, 0, 0, 0)),
            pl.BlockSpec((1,) + w_out_bf16.shape[1:], lambda i: (layer, 0, 0)),
            xs,
            pl.BlockSpec((1, 1, d), lambda i: (i // tiles_per_seq, 0, 0)),
            vec(d),
        ],
        out_specs=xs,
        scratch_shapes=[
            pltpu.VMEM((tm + POOL_HALO, w), F32),
            pltpu.VMEM((RET_HEADS, RET_HEAD_DIM, RET_HEAD_DIM), F32),
            pltpu.VMEM((RWKV_PAIRS, RWKV_PAIR_W, RWKV_PAIR_W), F32),
            pltpu.VMEM((1, w), F32), pltpu.VMEM((1, w), F32), pltpu.VMEM((1, w), F32),
            pltpu.VMEM((1, 2 * RWKV_LORA), F32),
        ],
        compiler_params=_params("arbitrary"),
        name="mix_merge",
    )(proj, proj, pool_w_bf16, pool_scale.reshape(1, w),
      proj, proj, proj, proj, cosf, sinf, mask, q_dec, k_tail, chunk_decay,
      ret_norm_g.reshape(1, w),
      proj, proj, proj, proj, proj,
      row(prm["mu_r"]), row(prm["mu_k"]), row(prm["mu_v"]), row(prm["mu_lo"]),
      row(prm["w0"]), prm["w2"], row(prm["a0"]), prm["a2"],
      row(prm["k_k"]), row(prm["k_a"]), row(prm["r_k"]), row(prm["ln_w"]), row(prm["ln_b"]),
      seg,
      proj, proj, proj, w_branch_bf16, w_out_bf16, x2, gate, final_g.reshape(1, d))


def _cast_kernel(x_ref, o_ref):
    o_ref[...] = x_ref[...].astype(o_ref.dtype)


def _to_bf16(w):
    lead = w.shape[:-2]
    tail = w.shape[-2:]
    flat = w.reshape((-1,) + tail)
    per_step = math.gcd(flat.shape[0], max(1, CAST_BLOCK_BYTES // (4 * tail[0] * tail[1])))
    spec = pl.BlockSpec((per_step,) + tail, lambda i: (i, 0, 0))
    out = pl.pallas_call(
        _cast_kernel,
        out_shape=jax.ShapeDtypeStruct(flat.shape, BF16),
        grid=(flat.shape[0] // per_step,),
        in_specs=[spec],
        out_specs=spec,
        compiler_params=_params("arbitrary"),
        name="weights_to_bf16",
    )(flat)
    return out.reshape(lead + tail)


def _wprep_kernel(w_ref, o_ref):
    w = BRANCH_W
    lo = RWKV_LORA
    r_end = 7 * w
    rest_src = r_end + 2 * w + 2 * lo
    rest_dst = r_end + 2 * w
    rows = w_ref.shape[1]
    o_ref[0, :, 0:r_end] = w_ref[0, :, 0:r_end].astype(BF16)
    mid = w_ref[0, :, r_end:rest_src]
    o_ref[0, :, r_end:r_end + w] = mid[:, lo:lo + w].astype(BF16)
    o_ref[0, :, r_end + w:rest_dst] = mid[:, lo + w:lo + 2 * w].astype(BF16)
    o_ref[0, :, rest_dst:rest_dst + (D_IN - rest_src)] = w_ref[0, :, rest_src:D_IN].astype(BF16)
    tail = jnp.concatenate([mid[:, 0:lo], mid[:, lo + 2 * w:2 * lo + 2 * w],
                            jnp.zeros((rows, D_IN_PAD - D_IN), F32)], axis=1)
    o_ref[0, :, D_IN - 2 * lo:D_IN_PAD] = tail.astype(BF16)


def _prepare_w_in(w_in):
    depth, d, d_in = w_in.shape
    return pl.pallas_call(
        _wprep_kernel,
        out_shape=jax.ShapeDtypeStruct((depth, d, D_IN_PAD), BF16),
        grid=(depth, d // W_PREP_ROWS),
        in_specs=[pl.BlockSpec((1, W_PREP_ROWS, d_in), lambda l, i: (l, i, 0))],
        out_specs=pl.BlockSpec((1, W_PREP_ROWS, D_IN_PAD), lambda l, i: (l, i, 0)),
        compiler_params=_params("arbitrary", "arbitrary"),
        name="w_in_prep",
    )(w_in)


def kernel(x, c, positions, norm_g, w_ada, b_ada, w_in, pool_w, pool_scale, ret_norm_g,
           rwkv_shift_mu, rwkv_w0, rwkv_w2, rwkv_a0, rwkv_a2, rwkv_k_k, rwkv_k_a, rwkv_r_k,
           rwkv_ln_w, rwkv_ln_b, w_branch, w_out, final_g):
    batch, seq, d = x.shape
    depth = w_in.shape[0]
    ntok = batch * seq
    w = BRANCH_W
    assert d == D_MODEL and w_in.shape[-1] == D_IN
    assert seq % IN_TM == 0 and seq % MIX_TM == 0 and MIX_TM == RWKV_GROUP * RWKV_CHUNK

    mod_rows = 8
    c_pad = jnp.zeros((mod_rows, d), F32).at[:batch].set(c)
    mod = _modulation(c_pad, w_ada, b_ada)[:, :batch]
    cosf, sinf = _rope_tables(positions.reshape(ntok, 1))

    w_in_p = _prepare_w_in(w_in)
    pool_w_b = _to_bf16(pool_w)
    w_branch_b = _to_bf16(w_branch)
    w_out_b = _to_bf16(w_out)
    lora_pad = jnp.zeros((depth, RWKV_LORA, w), F32)
    w2_pad = jnp.concatenate([rwkv_w2, lora_pad], axis=1)
    a2_pad = jnp.concatenate([lora_pad, rwkv_a2], axis=1)
    mu = rwkv_shift_mu
    mu_lo = jnp.concatenate([mu[:, w:w + RWKV_LORA], mu[:, 3 * w + RWKV_LORA:]], axis=1)

    x2 = x.reshape(ntok, d)
    for l in range(depth):
        shift = mod[l, :, 0:d].reshape(batch, 1, d)
        scale = mod[l, :, d:2 * d].reshape(batch, 1, d)
        gate = mod[l, :, 2 * d:].reshape(batch, 1, d)
        proj = _in_projection(x2, norm_g[l], shift, scale, w_in_p, l, seq)
        prm = dict(mu_r=mu[l, 0:w], mu_k=mu[l, w + RWKV_LORA:2 * w + RWKV_LORA],
                   mu_v=mu[l, 2 * w + RWKV_LORA:3 * w + RWKV_LORA], mu_lo=mu_lo[l],
                   w0=rwkv_w0[l], w2=w2_pad[l], a0=rwkv_a0[l], a2=a2_pad[l],
                   k_k=rwkv_k_k[l], k_a=rwkv_k_a[l], r_k=rwkv_r_k[l],
                   ln_w=rwkv_ln_w[l], ln_b=rwkv_ln_b[l])
        x2 = _mix_merge(proj, prm, cosf, sinf, pool_w_b, pool_scale[l], ret_norm_g[l], w_branch_b,
                        w_out_b, l, x2, gate, final_g, seq, final_norm=(l == depth - 1))
    return x2.reshape(batch, seq, d)
```
